```python
import jax, jax.numpy as jnp
from jax import lax
import numpy as np

D_MODEL = 2048
BATCH = 2
SEQ = 8192
DEPTH = 4
DEC_BATCH = 2
DEC_SEQ = 16384
PAST_LEN = 128

MIX_DIM = D_MODEL
DN_HEADS = 8
DN_HEAD_DIM = 128
DN_DIM = DN_HEADS * DN_HEAD_DIM
SC_GROUPS = 8
SC_DIM = MIX_DIM - DN_DIM
SC_GROUP_DIM = SC_DIM // SC_GROUPS
CONV_WIDTH = 3
CHUNK = 64
D_FF = 5632
N_GATE_COLS = 4 * DN_HEADS
IN_COLS = 3 * DN_DIM + DN_DIM + N_GATE_COLS + 3 * SC_DIM
NORM_EPS = 1e-6
L2_EPS = 1e-6

kernel_name = 'hybrid_deltanet_shortconv_encoder'


def rms_norm(x, w):
    xf = x.astype(jnp.float32)
    y = xf * lax.rsqrt(jnp.mean(xf * xf, axis=-1, keepdims=True) + NORM_EPS)
    return (y * w.astype(jnp.float32)).astype(x.dtype)


def l2_norm(x):
    return x * lax.rsqrt(jnp.sum(x * x, axis=-1, keepdims=True) + L2_EPS)


def dwconv_centred(x, w):
    K = w.shape[0]
    p = K // 2
    T = x.shape[1]
    xp = jnp.pad(x, ((0, 0), (p, p), (0, 0)))
    return sum(xp[:, i:i + T] * w[i] for i in range(K))


def gated_delta_chunked(q, k, v, g, beta):
    N, T, H, Dk = q.shape
    Dv = v.shape[-1]
    NC = T // CHUNK

    def to_chunks(t):
        return t.reshape(N, NC, CHUNK, H, -1).transpose(1, 0, 3, 2, 4)

    q = to_chunks(q) * (Dk ** -0.5)
    k = to_chunks(k)
    v = to_chunks(v)
    g = g.reshape(N, NC, CHUNK, H).transpose(1, 0, 3, 2)
    beta = beta.reshape(N, NC, CHUNK, H).transpose(1, 0, 3, 2)
    g = jnp.cumsum(g, axis=-1)

    tri = jnp.tril(jnp.ones((CHUNK, CHUNK), dtype=bool))
    strict = jnp.tril(jnp.ones((CHUNK, CHUNK), dtype=bool), -1)
    eye = jnp.eye(CHUNK, dtype=q.dtype)
    diff = g[..., :, None] - g[..., None, :]
    decay = jnp.where(tri, jnp.exp(jnp.where(tri, diff, 0.0)), 0.0)

    kb = k * beta[..., None]
    vb = v * beta[..., None]
    a_kk = jnp.where(strict, jnp.einsum('znhcd,znhed->znhce', kb, k) * decay, 0.0)
    lower = a_kk + eye
    u = lax.linalg.triangular_solve(lower, vb, left_side=True, lower=True, unit_diagonal=True)
    w = lax.linalg.triangular_solve(lower, kb * jnp.exp(g)[..., None], left_side=True, lower=True, unit_diagonal=True)

    a_qk = jnp.einsum('znhcd,znhed->znhce', q, k) * decay
    qg = q * jnp.exp(g)[..., None]
    kdec = k * jnp.exp(g[..., -1:] - g)[..., None]
    glast = jnp.exp(g[..., -1])

    def step(S, xs):
        qg_i, kdec_i, u_i, w_i, a_i, gl_i = xs
        v_new = u_i - jnp.einsum('nhcd,nhde->nhce', w_i, S)
        o_i = jnp.einsum('nhcd,nhde->nhce', qg_i, S) + jnp.einsum('nhce,nhed->nhcd', a_i, v_new)
        S = S * gl_i[..., None, None] + jnp.einsum('nhcd,nhce->nhde', kdec_i, v_new)
        return S, o_i

    S0 = jnp.zeros((N, H, Dk, Dv), dtype=q.dtype)
    _, o = lax.scan(step, S0, (qg, kdec, u, w, a_qk, glast))
    return o.transpose(1, 0, 3, 2, 4).reshape(N, T, H, Dv)


def token_mixer(h, w_in, conv_qkv, a_log, dt_bias, dn_norm, conv_sc, sc_norm, w_out):
    B, T, _ = h.shape
    p = h @ w_in
    qkv, z, gl, sc = jnp.split(p, [3 * DN_DIM, 4 * DN_DIM, 4 * DN_DIM + N_GATE_COLS], axis=-1)

    qkv = jax.nn.silu(dwconv_centred(qkv, conv_qkv)).astype(jnp.float32)
    q, k, v = jnp.split(qkv, 3, axis=-1)
    q = l2_norm(q.reshape(B, T, DN_HEADS, DN_HEAD_DIM))
    k = l2_norm(k.reshape(B, T, DN_HEADS, DN_HEAD_DIM))
    v = v.reshape(B, T, DN_HEADS, DN_HEAD_DIM)
    gl = gl.astype(jnp.float32).reshape(B, T, 4, DN_HEADS)
    a_gate = gl[:, :, 0:2]
    b_gate = gl[:, :, 2:4]
    g = -jnp.exp(a_log.astype(jnp.float32)) * jax.nn.softplus(a_gate + dt_bias.astype(jnp.float32))
    beta = jax.nn.sigmoid(b_gate)
    flip = lambda t: jnp.flip(t, axis=1)
    q2 = jnp.concatenate([q, flip(q)], axis=0)
    k2 = jnp.concatenate([k, flip(k)], axis=0)
    v2 = jnp.concatenate([v, flip(v)], axis=0)
    g2 = jnp.concatenate([g[:, :, 0], flip(g[:, :, 1])], axis=0)
    beta2 = jnp.concatenate([beta[:, :, 0], flip(beta[:, :, 1])], axis=0)
    o2 = gated_delta_chunked(q2, k2, v2, g2, beta2)
    o = o2[:B] + flip(o2[B:])
    o = rms_norm(o, dn_norm) * jax.nn.silu(z.astype(jnp.float32).reshape(B, T, DN_HEADS, DN_HEAD_DIM))
    o_dn = o.reshape(B, T, DN_DIM).astype(h.dtype)

    b_g, c_g, x_sc = jnp.split(sc, 3, axis=-1)
    y = b_g * dwconv_centred(c_g * x_sc, conv_sc)
    y = rms_norm(y.reshape(B, T, SC_GROUPS, SC_GROUP_DIM), jnp.ones((SC_GROUP_DIM,), dtype=y.dtype))
    y_sc = y.reshape(B, T, SC_DIM) * sc_norm

    return jnp.concatenate([o_dn, y_sc], axis=-1) @ w_out


def conv_glu_ffn(h, w_up, conv_ffn, w_down):
    a, b = jnp.split(h @ w_up, 2, axis=-1)
    a = dwconv_centred(a, conv_ffn)
    return (jax.nn.silu(a) * b) @ w_down


def encoder_layer(x, norm_mix_pre, w_in, conv_qkv, a_log, dt_bias, dn_norm, conv_sc, sc_norm, w_out,
                  norm_mix_post, norm_ffn_pre, w_up, conv_ffn, w_down, norm_ffn_post):
    h = rms_norm(x, norm_mix_pre)
    x = x + rms_norm(token_mixer(h, w_in, conv_qkv, a_log, dt_bias, dn_norm, conv_sc, sc_norm, w_out), norm_mix_post)
    h = rms_norm(x, norm_ffn_pre)
    x = x + rms_norm(conv_glu_ffn(h, w_up, conv_ffn, w_down), norm_ffn_post)
    return x


def setup_inputs(seed: int = 0) -> dict:
    key = jax.random.key(seed)
    ks = jax.random.split(key, 20)
    f32 = jnp.float32
    nrm = lambda k, shape, s: jax.random.normal(k, shape, f32) * s
    gain = lambda k, shape: 1.0 + 0.05 * jax.random.normal(k, shape, f32)
    L = DEPTH
    dt = jnp.exp(jax.random.uniform(ks[5], (L, 2, DN_HEADS), f32, np.log(1e-3), np.log(1e-1)))
    return {
        'x_prompt': jax.random.normal(ks[0], (BATCH, SEQ, D_MODEL), f32),
        'x_sample': jax.random.normal(ks[1], (DEC_BATCH, DEC_SEQ, D_MODEL), f32),
        'norm_mix_pre': gain(ks[2], (L, D_MODEL)),
        'w_in': nrm(ks[3], (L, D_MODEL, IN_COLS), D_MODEL ** -0.5),
        'conv_qkv': nrm(ks[4], (L, CONV_WIDTH, 3 * DN_DIM), CONV_WIDTH ** -0.5),
        'a_log': jnp.log(jax.random.uniform(ks[6], (L, 2, DN_HEADS), f32, 1.0, 16.0)),
        'dt_bias': dt + jnp.log(-jnp.expm1(-dt)),
        'dn_norm': gain(ks[7], (L, DN_HEAD_DIM)),
        'conv_sc': nrm(ks[8], (L, CONV_WIDTH, SC_DIM), CONV_WIDTH ** -0.5),
        'sc_norm': gain(ks[9], (L, SC_DIM)),
        'w_out': nrm(ks[10], (L, MIX_DIM, D_MODEL), MIX_DIM ** -0.5),
        'norm_mix_post': gain(ks[11], (L, D_MODEL)),
        'norm_ffn_pre': gain(ks[12], (L, D_MODEL)),
        'w_up': nrm(ks[13], (L, D_MODEL, 2 * D_FF), D_MODEL ** -0.5),
        'conv_ffn': nrm(ks[14], (L, CONV_WIDTH, D_FF), CONV_WIDTH ** -0.5),
        'w_down': nrm(ks[15], (L, D_FF, D_MODEL), D_FF ** -0.5),
        'norm_ffn_post': gain(ks[16], (L, D_MODEL)),
    }


def reference(x_prompt, x_sample, norm_mix_pre, w_in, conv_qkv, a_log, dt_bias, dn_norm, conv_sc, sc_norm,
              w_out, norm_mix_post, norm_ffn_pre, w_up, conv_ffn, w_down, norm_ffn_post):
    y_prompt = x_prompt
    y_sample = x_sample
    for l in range(DEPTH):
        layer_params = (norm_mix_pre[l], w_in[l], conv_qkv[l], a_log[l], dt_bias[l], dn_norm[l], conv_sc[l],
                        sc_norm[l], w_out[l], norm_mix_post[l], norm_ffn_pre[l], w_up[l], conv_ffn[l], w_down[l],
                        norm_ffn_post[l])
        y_prompt = encoder_layer(y_prompt, *layer_params)
        y_sample = encoder_layer(y_sample, *layer_params)
    return (y_prompt, y_sample)
```

```python
import functools

import jax
import jax.numpy as jnp
from jax import lax
from jax.experimental import pallas as pl
from jax.experimental.pallas import tpu as pltpu

D_MODEL = 2048
DN_HEADS = 8
DN_HEAD_DIM = 128
DN_DIM = DN_HEADS * DN_HEAD_DIM
SC_GROUPS = 8
SC_GROUP_DIM = 128
SC_DIM = SC_GROUPS * SC_GROUP_DIM
D_FF = 5632
CHUNK = 64
N_GATE_COLS = 4 * DN_HEADS
NORM_EPS = 1e-6
L2_EPS = 1e-6
Q_SCALE = DN_HEAD_DIM ** -0.5

LANES = 128
HALO = 16
VMEM_LIMIT = 56 * 1024 * 1024

F32 = jnp.float32
BF16 = jnp.bfloat16
NT_DIMS = (((1,), (1,)), ((), ()))
TN_DIMS = (((0,), (0,)), ((), ()))


def _sigmoid(x):
    return 1.0 / (1.0 + jnp.exp(-x))


def _silu(x):
    return x * _sigmoid(x)


def _softplus(x):
    return jnp.maximum(x, 0.0) + jnp.log1p(jnp.exp(-jnp.abs(x)))


def _rms(x):
    return x * lax.rsqrt(jnp.mean(x * x, axis=-1, keepdims=True) + NORM_EPS)


def _bdot(a, b):
    return jnp.dot(a, b, preferred_element_type=F32)


def _conv3_rows(p, cw, tm):
    rows = p.shape[0]
    prev = pltpu.roll(p, 1, 0)[HALO:HALO + tm]
    nxt = pltpu.roll(p, rows - 1, 0)[HALO:HALO + tm]
    mid = p[HALO:HALO + tm]
    return prev * cw[0:1] + mid * cw[1:2] + nxt * cw[2:3]


def _fill_halo(hx_ref, h_ref, hp_ref, hn_ref, tm, first, last):
    hp = hp_ref[...]
    hn = hn_ref[...]
    hx_ref[0:HALO, :] = jnp.where(first, jnp.zeros_like(hp), hp)
    hx_ref[HALO:HALO + tm, :] = h_ref[...]
    hx_ref[HALO + tm:HALO + tm + HALO, :] = jnp.where(last, jnp.zeros_like(hn), hn)


def _halo_specs(tm, t_len, width):
    per = tm // HALO
    nblk = t_len // HALO
    main = pl.BlockSpec((None, tm, width), lambda n, i, j: (n, i, 0))
    prev = pl.BlockSpec((None, HALO, width), lambda n, i, j: (n, jnp.maximum(i * per - 1, 0), 0))
    nxt = pl.BlockSpec((None, HALO, width), lambda n, i, j: (n, jnp.minimum((i + 1) * per, nblk - 1), 0))
    return main, prev, nxt


def _params(sem):
    return pltpu.CompilerParams(dimension_semantics=sem, vmem_limit_bytes=VMEM_LIMIT)


def _prenorm_kernel(x_ref, w_ref, o_ref):
    o_ref[...] = (_rms(x_ref[...]) * w_ref[...]).astype(o_ref.dtype)


def _prenorm(x, w, tm):
    n, t, d = x.shape
    return pl.pallas_call(
        _prenorm_kernel,
        out_shape=jax.ShapeDtypeStruct((n, t, d), BF16),
        grid=(n, t // tm),
        in_specs=[pl.BlockSpec((None, tm, d), lambda a, i: (a, i, 0)),
                  pl.BlockSpec((1, d), lambda a, i: (0, 0))],
        out_specs=pl.BlockSpec((None, tm, d), lambda a, i: (a, i, 0)),
        compiler_params=_params(("parallel", "parallel")),
        name="prenorm",
    )(x, w.reshape(1, d))


def _qkv_kernel(h_ref, hp_ref, hn_ref, w_ref, cw_ref, o_ref, hx_ref, *, tm, tn):
    i = pl.program_id(1)
    j = pl.program_id(2)

    @pl.when(j == 0)
    def _():
        _fill_halo(hx_ref, h_ref, hp_ref, hn_ref, tm, i == 0, i == pl.num_programs(1) - 1)

    p = _bdot(hx_ref[...], w_ref[...])
    y = _silu(_conv3_rows(p, cw_ref[...], tm))
    norm_tiles = 2 * DN_DIM // tn

    @pl.when(j < norm_tiles)
    def _():
        for hh in range(tn // DN_HEAD_DIM):
            sl = slice(hh * DN_HEAD_DIM, (hh + 1) * DN_HEAD_DIM)
            yh = y[:, sl]
            ss = jnp.sum(yh * yh, axis=-1, keepdims=True)
            o_ref[:, sl] = (yh * lax.rsqrt(ss + L2_EPS)).astype(o_ref.dtype)

    @pl.when(j >= norm_tiles)
    def _():
        o_ref[...] = y.astype(o_ref.dtype)


def _qkv_proj(h, w, cw, tm, tn=512):
    n, t, d = h.shape
    cols = w.shape[1]
    main, prev, nxt = _halo_specs(tm, t, d)
    return pl.pallas_call(
        functools.partial(_qkv_kernel, tm=tm, tn=tn),
        out_shape=jax.ShapeDtypeStruct((n, t, cols), BF16),
        grid=(n, t // tm, cols // tn),
        in_specs=[main, prev, nxt,
                  pl.BlockSpec((d, tn), lambda a, i, j: (0, j)),
                  pl.BlockSpec((3, tn), lambda a, i, j: (0, j))],
        out_specs=pl.BlockSpec((None, tm, tn), lambda a, i, j: (a, i, j)),
        scratch_shapes=[pltpu.VMEM((tm + 2 * HALO, d), BF16)],
        compiler_params=_params(("parallel", "parallel", "arbitrary")),
        name="qkv_proj",
    )(h, h, h, w, cw)


def _zg_kernel(h_ref, w_ref, alog_ref, dtb_ref, z_ref, g_ref):
    p = _bdot(h_ref[...], w_ref[...])
    z_ref[...] = _silu(p[:, :DN_DIM]).astype(z_ref.dtype)
    gl = p[:, DN_DIM:]
    lane = lax.broadcasted_iota(jnp.int32, gl.shape, 1)
    g = -jnp.exp(alog_ref[...]) * _softplus(gl + dtb_ref[...])
    g_ref[...] = jnp.where(lane < 2 * DN_HEADS, g, _sigmoid(gl))


def _zg_proj(h, w, alog, dtb, tm):
    n, t, d = h.shape
    cols = w.shape[1]
    return pl.pallas_call(
        _zg_kernel,
        out_shape=(jax.ShapeDtypeStruct((n, t, DN_DIM), BF16),
                   jax.ShapeDtypeStruct((n, t, LANES), F32)),
        grid=(n, t // tm),
        in_specs=[pl.BlockSpec((None, tm, d), lambda a, i: (a, i, 0)),
                  pl.BlockSpec((d, cols), lambda a, i: (0, 0)),
                  pl.BlockSpec((1, LANES), lambda a, i: (0, 0)),
                  pl.BlockSpec((1, LANES), lambda a, i: (0, 0))],
        out_specs=(pl.BlockSpec((None, tm, DN_DIM), lambda a, i: (a, i, 0)),
                   pl.BlockSpec((None, tm, LANES), lambda a, i: (a, i, 0))),
        compiler_params=_params(("parallel", "parallel")),
        name="zg_proj",
    )(h, w, alog, dtb)


def _sc_kernel(h_ref, hp_ref, hn_ref, w_ref, cw_ref, gain_ref, o_ref, hx_ref, *, tm, groups):
    i = pl.program_id(1)
    j = pl.program_id(2)

    @pl.when(j == 0)
    def _():
        _fill_halo(hx_ref, h_ref, hp_ref, hn_ref, tm, i == 0, i == pl.num_programs(1) - 1)

    p = _bdot(hx_ref[...], w_ref[...])
    gd = SC_GROUP_DIM
    for gi in range(groups):
        base = gi * 3 * gd
        b = p[HALO:HALO + tm, base:base + gd]
        cx = p[:, base + gd:base + 2 * gd] * p[:, base + 2 * gd:base + 3 * gd]
        sl = slice(gi * gd, (gi + 1) * gd)
        y = b * _conv3_rows(cx, cw_ref[:, sl], tm)
        o_ref[:, sl] = (_rms(y) * gain_ref[:, sl]).astype(o_ref.dtype)


def _sc_proj(h, w, cw, gain, tm, groups=2):
    n, t, d = h.shape
    main, prev, nxt = _halo_specs(tm, t, d)
    tn = groups * 3 * SC_GROUP_DIM
    to = groups * SC_GROUP_DIM
    return pl.pallas_call(
        functools.partial(_sc_kernel, tm=tm, groups=groups),
        out_shape=jax.ShapeDtypeStruct((n, t, SC_DIM), BF16),
        grid=(n, t // tm, SC_GROUPS // groups),
        in_specs=[main, prev, nxt,
                  pl.BlockSpec((d, tn), lambda a, i, j: (0, j)),
                  pl.BlockSpec((3, to), lambda a, i, j: (0, j)),
                  pl.BlockSpec((1, to), lambda a, i, j: (0, j))],
        out_specs=pl.BlockSpec((None, tm, to), lambda a, i, j: (a, i, j)),
        scratch_shapes=[pltpu.VMEM((tm + 2 * HALO, d), BF16)],
        compiler_params=_params(("parallel", "parallel", "arbitrary")),
        name="sc_proj",
    )(h, h, h, w, cw, gain.reshape(1, SC_DIM))


def _delta_kernel(q_ref, k_ref, v_ref, gc_ref, gr_ref, o_ref, s_ref, *, reverse, tb):
    @pl.when(pl.program_id(1) == 0)
    def _():
        s_ref[...] = jnp.zeros_like(s_ref)

    nch = tb // CHUNK
    r = lax.broadcasted_iota(jnp.int32, (CHUNK, CHUNK), 0)
    c = lax.broadcasted_iota(jnp.int32, (CHUNK, CHUNK), 1)
    if reverse:
        mle, strict = r <= c, r < c
    else:
        mle, strict = r >= c, r > c
    eye = (r == c).astype(F32)
    mle_f = mle.astype(F32)
    mle_t = (r >= c if reverse else r <= c).astype(F32)
    goff = DN_HEADS if reverse else 0
    boff = 2 * DN_HEADS + goff

    def chunk_body(jj, carry):
        cc = (nch - 1 - jj) if reverse else jj
        rows = pl.ds(pl.multiple_of(cc * CHUNK, CHUNK), CHUNK)
        gcol = gc_ref[rows, :]
        grow = gr_ref[cc]
        g_c = gcol[:, goff:goff + DN_HEADS]
        b_c = gcol[:, boff:boff + DN_HEADS]
        g_r = grow[goff:goff + DN_HEADS, :]
        b_r = grow[boff:boff + DN_HEADS, :]
        cum_c = jnp.dot(mle_f, g_c, precision=lax.Precision.HIGHEST, preferred_element_type=F32)
        cum_r = jnp.dot(g_r, mle_t, precision=lax.Precision.HIGHEST, preferred_element_type=F32)
        g_last = jnp.sum(g_r, axis=1, keepdims=True)
        for hh in range(DN_HEADS):
            sl = slice(hh * DN_HEAD_DIM, (hh + 1) * DN_HEAD_DIM)
            q = q_ref[rows, sl]
            k = k_ref[rows, sl]
            v = v_ref[rows, sl]
            gc = cum_c[:, hh:hh + 1]
            gr = cum_r[hh:hh + 1, :]
            bc = b_c[:, hh:hh + 1]
            br = b_r[hh:hh + 1, :]
            gl = g_last[hh:hh + 1, :]
            decay = jnp.where(mle, jnp.exp(jnp.where(mle, gc - gr, 0.0)), 0.0)
            kf = k.astype(F32)
            kb = (kf * bc).astype(BF16)
            kk = lax.dot_general(kb, k, NT_DIMS, preferred_element_type=F32)
            qk = lax.dot_general(q, k, NT_DIMS, preferred_element_type=F32)
            a_qk = (qk * decay * Q_SCALE).astype(BF16)
            pw = jnp.where(strict, -kk * decay, 0.0)
            tinv = eye + pw
            pb = pw.astype(BF16)
            pw = _bdot(pb, pb)
            for _ in range(4):
                both = _bdot(jnp.concatenate([tinv, pw], axis=0).astype(BF16), pw.astype(BF16))
                tinv = tinv + both[:CHUNK]
                pw = both[CHUNK:]
            tinv = tinv + _bdot(tinv.astype(BF16), pw.astype(BF16))
            u = _bdot((tinv * br).astype(BF16), v)
            w = _bdot((tinv * (br * jnp.exp(gr))).astype(BF16), k)
            s = s_ref[hh]
            sb = s.astype(BF16)
            v_new = u - _bdot(w.astype(BF16), sb)
            vb = v_new.astype(BF16)
            qg = (q.astype(F32) * (jnp.exp(gc) * Q_SCALE)).astype(BF16)
            o_ref[rows, sl] = _bdot(qg, sb) + _bdot(a_qk, vb)
            kdec = (kf * jnp.exp(gl - gc)).astype(BF16)
            s_ref[hh] = s * jnp.exp(gl) + lax.dot_general(kdec, vb, TN_DIMS, preferred_element_type=F32)
        return carry

    lax.fori_loop(0, nch, chunk_body, 0)


def _delta_scan(qkv, gates, gates_t, reverse, tb):
    n, t, _ = qkv.shape
    nb = t // tb
    nch = tb // CHUNK
    blk = (lambda i: nb - 1 - i) if reverse else (lambda i: i)
    return pl.pallas_call(
        functools.partial(_delta_kernel, reverse=reverse, tb=tb),
        out_shape=jax.ShapeDtypeStruct((n, t, DN_DIM), F32),
        grid=(n, nb),
        in_specs=[pl.BlockSpec((None, tb, DN_DIM), lambda a, i: (a, blk(i), 0)),
                  pl.BlockSpec((None, tb, DN_DIM), lambda a, i: (a, blk(i), 1)),
                  pl.BlockSpec((None, tb, DN_DIM), lambda a, i: (a, blk(i), 2)),
                  pl.BlockSpec((None, tb, LANES), lambda a, i: (a, blk(i), 0)),
                  pl.BlockSpec((None, nch, N_GATE_COLS, CHUNK), lambda a, i: (a, blk(i), 0, 0))],
        out_specs=pl.BlockSpec((None, tb, DN_DIM), lambda a, i: (a, blk(i), 0)),
        scratch_shapes=[pltpu.VMEM((DN_HEADS, DN_HEAD_DIM, DN_HEAD_DIM), F32)],
        compiler_params=_params(("parallel", "arbitrary")),
        name="delta_bwd" if reverse else "delta_fwd",
    )(qkv, qkv, qkv, gates, gates_t)


def _out_kernel(of_ref, ob_ref, zs_ref, ysc_ref, x_ref, w_ref, dnw_ref, npost_ref, nnext_ref,
                xo_ref, hn_ref, odn_ref):
    for hh in range(DN_HEADS):
        sl = slice(hh * DN_HEAD_DIM, (hh + 1) * DN_HEAD_DIM)
        o = of_ref[:, sl] + ob_ref[:, sl]
        y = _rms(o) * dnw_ref[...] * zs_ref[:, sl].astype(F32)
        odn_ref[:, sl] = y.astype(odn_ref.dtype)
    m = _bdot(odn_ref[...], w_ref[0:DN_DIM, :]) + _bdot(ysc_ref[...], w_ref[DN_DIM:, :])
    xn = x_ref[...] + _rms(m) * npost_ref[...]
    xo_ref[...] = xn
    hn_ref[...] = (_rms(xn) * nnext_ref[...]).astype(hn_ref.dtype)


def _out_proj(o_f, o_b, zs, ysc, x, w, dn_w, n_post, n_next, tm):
    n, t, d = x.shape
    row = lambda width: pl.BlockSpec((None, tm, width), lambda a, i: (a, i, 0))
    vec = lambda width: pl.BlockSpec((1, width), lambda a, i: (0, 0))
    return pl.pallas_call(
        _out_kernel,
        out_shape=(jax.ShapeDtypeStruct((n, t, d), F32), jax.ShapeDtypeStruct((n, t, d), BF16)),
        grid=(n, t // tm),
        in_specs=[row(DN_DIM), row(DN_DIM), row(DN_DIM), row(SC_DIM), row(d),
                  pl.BlockSpec((d, d), lambda a, i: (0, 0)),
                  vec(DN_HEAD_DIM), vec(d), vec(d)],
        out_specs=(row(d), row(d)),
        scratch_shapes=[pltpu.VMEM((tm, DN_DIM), BF16)],
        compiler_params=_params(("parallel", "parallel")),
        name="out_proj",
    )(o_f, o_b, zs, ysc, x, w, dn_w.reshape(1, -1), n_post.reshape(1, d), n_next.reshape(1, d))


def _ffn_kernel(h_ref, hp_ref, hn_ref, wa_ref, wb_ref, cw_ref, wd_ref, x_ref, npost_ref, nnext_ref,
                xo_ref, ho_ref, hx_ref, acc_ref, *, tm):
    i = pl.program_id(1)
    f = pl.program_id(2)

    @pl.when(f == 0)
    def _():
        _fill_halo(hx_ref, h_ref, hp_ref, hn_ref, tm, i == 0, i == pl.num_programs(1) - 1)
        acc_ref[...] = jnp.zeros_like(acc_ref)

    a = _bdot(hx_ref[...], wa_ref[...])
    b = _bdot(h_ref[...], wb_ref[...])
    act = (_silu(_conv3_rows(a, cw_ref[...], tm)) * b).astype(BF16)
    acc_ref[...] += _bdot(act, wd_ref[...])

    @pl.when(f == pl.num_programs(2) - 1)
    def _():
        xn = x_ref[...] + _rms(acc_ref[...]) * npost_ref[...]
        xo_ref[...] = xn
        ho_ref[...] = (_rms(xn) * nnext_ref[...]).astype(ho_ref.dtype)


def _ffn(h, x, w_up, cw, w_down, n_post, n_next, tm, tf=512):
    n, t, d = x.shape
    nf = D_FF // tf
    main, prev, nxt = _halo_specs(tm, t, d)
    row = lambda: pl.BlockSpec((None, tm, d), lambda a, i, f: (a, i, 0))
    vec = lambda: pl.BlockSpec((1, d), lambda a, i, f: (0, 0))
    return pl.pallas_call(
        functools.partial(_ffn_kernel, tm=tm),
        out_shape=(jax.ShapeDtypeStruct((n, t, d), F32), jax.ShapeDtypeStruct((n, t, d), BF16)),
        grid=(n, t // tm, nf),
        in_specs=[main, prev, nxt,
                  pl.BlockSpec((d, tf), lambda a, i, f: (0, f)),
                  pl.BlockSpec((d, tf), lambda a, i, f: (0, f + nf)),
                  pl.BlockSpec((3, tf), lambda a, i, f: (0, f)),
                  pl.BlockSpec((tf, d), lambda a, i, f: (f, 0)),
                  row(), vec(), vec()],
        out_specs=(row(), row()),
        scratch_shapes=[pltpu.VMEM((tm + 2 * HALO, d), BF16), pltpu.VMEM((tm, d), F32)],
        compiler_params=_params(("parallel", "parallel", "arbitrary")),
        name="ffn",
    )(h, h, h, w_up, w_up, cw, w_down, x, n_post.reshape(1, d), n_next.reshape(1, d))


def _tiles(t):
    return min(512, t), min(256, t)


def _prepare(w_in, a_log, dt_bias):
    depth = w_in.shape[0]
    w_qkv = w_in[:, :, :3 * DN_DIM].astype(BF16)
    z_end = 4 * DN_DIM
    g_end = z_end + N_GATE_COLS
    pad = jnp.zeros((depth, D_MODEL, LANES - N_GATE_COLS), w_in.dtype)
    w_zg = jnp.concatenate([w_in[:, :, 3 * DN_DIM:g_end], pad], axis=-1).astype(BF16)
    w_sc = w_in[:, :, g_end:].reshape(depth, D_MODEL, 3, SC_GROUPS, SC_GROUP_DIM)
    w_sc = w_sc.transpose(0, 1, 3, 2, 4).reshape(depth, D_MODEL, 3 * SC_DIM).astype(BF16)
    vpad = jnp.zeros((depth, 1, LANES - 2 * DN_HEADS), F32)
    alog = jnp.concatenate([a_log.reshape(depth, 1, 2 * DN_HEADS).astype(F32), vpad], axis=-1)
    dtb = jnp.concatenate([dt_bias.reshape(depth, 1, 2 * DN_HEADS).astype(F32), vpad], axis=-1)
    return w_qkv, w_zg, w_sc, alog, dtb


def _trunk(x, prm):
    (norm_mix_pre, w_qkv, w_zg, w_sc, alog, dtb, conv_qkv, dn_norm, conv_sc, sc_norm, w_out,
     norm_mix_post, norm_ffn_pre, w_up, conv_ffn, w_down, norm_ffn_post) = prm
    depth = w_qkv.shape[0]
    n, t, _ = x.shape
    tm, tb = _tiles(t)
    h = _prenorm(x, norm_mix_pre[0], tm)
    for l in range(depth):
        qkv = _qkv_proj(h, w_qkv[l], conv_qkv[l], tm)
        zs, gates = _zg_proj(h, w_zg[l], alog[l], dtb[l], tm)
        ysc = _sc_proj(h, w_sc[l], conv_sc[l], sc_norm[l], tm)
        gates_t = gates[:, :, :N_GATE_COLS].reshape(n, t // CHUNK, CHUNK, N_GATE_COLS).transpose(0, 1, 3, 2)
        o_f = _delta_scan(qkv, gates, gates_t, False, tb)
        o_b = _delta_scan(qkv, gates, gates_t, True, tb)
        x, h = _out_proj(o_f, o_b, zs, ysc, x, w_out[l], dn_norm[l], norm_mix_post[l], norm_ffn_pre[l], tm)
        n_next = norm_mix_pre[(l + 1) % depth]
        x, h = _ffn(h, x, w_up[l], conv_ffn[l], w_down[l], norm_ffn_post[l], n_next, tm)
    return x


def kernel(x_prompt, x_sample, norm_mix_pre, w_in, conv_qkv, a_log, dt_bias, dn_norm, conv_sc, sc_norm, w_out, norm_mix_post, norm_ffn_pre, w_up, conv_ffn, w_down, norm_ffn_post):
    w_qkv, w_zg, w_sc, alog, dtb = _prepare(w_in, a_log, dt_bias)
    prm = (norm_mix_pre, w_qkv, w_zg, w_sc, alog, dtb, conv_qkv, dn_norm, conv_sc, sc_norm,
           w_out.astype(BF16), norm_mix_post, norm_ffn_pre, w_up.astype(BF16), conv_ffn,
           w_down.astype(BF16), norm_ffn_post)
    return (_trunk(x_prompt, prm), _trunk(x_sample, prm))
```

```python
import functools

import jax
import jax.numpy as jnp
from jax import lax
from jax.experimental import pallas as pl
from jax.experimental.pallas import tpu as pltpu

D_MODEL = 2048
DN_HEADS = 8
DN_HEAD_DIM = 128
DN_DIM = DN_HEADS * DN_HEAD_DIM
SC_GROUPS = 8
SC_GROUP_DIM = 128
SC_DIM = SC_GROUPS * SC_GROUP_DIM
D_FF = 5632
CHUNK = 64
N_GATE_COLS = 4 * DN_HEADS
NORM_EPS = 1e-6
L2_EPS = 1e-6
Q_SCALE = DN_HEAD_DIM ** -0.5

LANES = 128
HALO = 16
VMEM_LIMIT = 56 * 1024 * 1024

F32 = jnp.float32
BF16 = jnp.bfloat16
NT_DIMS = (((1,), (1,)), ((), ()))
TN_DIMS = (((0,), (0,)), ((), ()))


def _sigmoid(x):
    return 1.0 / (1.0 + jnp.exp(-x))


def _silu(x):
    return x * _sigmoid(x)


def _softplus(x):
    return jnp.maximum(x, 0.0) + jnp.log1p(jnp.exp(-jnp.abs(x)))


def _rms(x):
    return x * lax.rsqrt(jnp.mean(x * x, axis=-1, keepdims=True) + NORM_EPS)


def _bdot(a, b):
    return jnp.dot(a, b, preferred_element_type=F32)


def _conv3_rows(p, cw, tm):
    rows = p.shape[0]
    prev = pltpu.roll(p, 1, 0)[HALO:HALO + tm]
    nxt = pltpu.roll(p, rows - 1, 0)[HALO:HALO + tm]
    mid = p[HALO:HALO + tm]
    return prev * cw[0:1] + mid * cw[1:2] + nxt * cw[2:3]


def _fill_halo(hx_ref, h_ref, hp_ref, hn_ref, tm, first, last):
    hp = hp_ref[...]
    hn = hn_ref[...]
    hx_ref[0:HALO, :] = jnp.where(first, jnp.zeros_like(hp), hp)
    hx_ref[HALO:HALO + tm, :] = h_ref[...]
    hx_ref[HALO + tm:HALO + tm + HALO, :] = jnp.where(last, jnp.zeros_like(hn), hn)


def _halo_specs(tm, t_len, width):
    per = tm // HALO
    nblk = t_len // HALO
    main = pl.BlockSpec((None, tm, width), lambda n, i, j: (n, i, 0))
    prev = pl.BlockSpec((None, HALO, width), lambda n, i, j: (n, jnp.maximum(i * per - 1, 0), 0))
    nxt = pl.BlockSpec((None, HALO, width), lambda n, i, j: (n, jnp.minimum((i + 1) * per, nblk - 1), 0))
    return main, prev, nxt


def _params(sem):
    return pltpu.CompilerParams(dimension_semantics=sem, vmem_limit_bytes=VMEM_LIMIT)


def _prenorm_kernel(x_ref, w_ref, o_ref):
    o_ref[...] = (_rms(x_ref[...]) * w_ref[...]).astype(o_ref.dtype)


def _prenorm(x, w, tm):
    n, t, d = x.shape
    return pl.pallas_call(
        _prenorm_kernel,
        out_shape=jax.ShapeDtypeStruct((n, t, d), BF16),
        grid=(n, t // tm),
        in_specs=[pl.BlockSpec((None, tm, d), lambda a, i: (a, i, 0)),
                  pl.BlockSpec((1, d), lambda a, i: (0, 0))],
        out_specs=pl.BlockSpec((None, tm, d), lambda a, i: (a, i, 0)),
        compiler_params=_params(("parallel", "parallel")),
        name="prenorm",
    )(x, w.reshape(1, d))


def _qkv_kernel(h_ref, hp_ref, hn_ref, w_ref, cw_ref, o_ref, hx_ref, *, tm, tn):
    i = pl.program_id(1)
    j = pl.program_id(2)

    @pl.when(j == 0)
    def _():
        _fill_halo(hx_ref, h_ref, hp_ref, hn_ref, tm, i == 0, i == pl.num_programs(1) - 1)

    p = _bdot(hx_ref[...], w_ref[...])
    y = _silu(_conv3_rows(p, cw_ref[...], tm))
    norm_tiles = 2 * DN_DIM // tn

    @pl.when(j < norm_tiles)
    def _():
        for hh in range(tn // DN_HEAD_DIM):
            sl = slice(hh * DN_HEAD_DIM, (hh + 1) * DN_HEAD_DIM)
            yh = y[:, sl]
            ss = jnp.sum(yh * yh, axis=-1, keepdims=True)
            o_ref[:, sl] = (yh * lax.rsqrt(ss + L2_EPS)).astype(o_ref.dtype)

    @pl.when(j >= norm_tiles)
    def _():
        o_ref[...] = y.astype(o_ref.dtype)


def _qkv_proj(h, w, cw, tm, tn=512):
    n, t, d = h.shape
    cols = w.shape[1]
    main, prev, nxt = _halo_specs(tm, t, d)
    return pl.pallas_call(
        functools.partial(_qkv_kernel, tm=tm, tn=tn),
        out_shape=jax.ShapeDtypeStruct((n, t, cols), BF16),
        grid=(n, t // tm, cols // tn),
        in_specs=[main, prev, nxt,
                  pl.BlockSpec((d, tn), lambda a, i, j: (0, j)),
                  pl.BlockSpec((3, tn), lambda a, i, j: (0, j))],
        out_specs=pl.BlockSpec((None, tm, tn), lambda a, i, j: (a, i, j)),
        scratch_shapes=[pltpu.VMEM((tm + 2 * HALO, d), BF16)],
        compiler_params=_params(("parallel", "parallel", "arbitrary")),
        name="qkv_proj",
    )(h, h, h, w, cw)


def _split3(x):
    hi = x.astype(BF16)
    r1 = x - hi.astype(F32)
    mid = r1.astype(BF16)
    lo = (r1 - mid.astype(F32)).astype(BF16)
    return hi, mid, lo


def _zg_kernel(h_ref, w_ref, alog_ref, dtb_ref, z_ref, g_ref, *, tm):
    p = _bdot(h_ref[...], w_ref[...])
    z_ref[...] = _silu(p[:, :DN_DIM]).astype(z_ref.dtype)
    gl = p[:, DN_DIM:]
    g = -jnp.exp(alog_ref[...]) * _softplus(gl + dtb_ref[...])
    beta = _sigmoid(gl)
    hi, mid, lo = _split3(g)
    r = lax.broadcasted_iota(jnp.int32, (CHUNK, 3 * CHUNK), 0)
    c = lax.broadcasted_iota(jnp.int32, (CHUNK, 3 * CHUNK), 1) % CHUNK
    m_fwd = (c <= r).astype(BF16)
    m_bwd = (c >= r).astype(BF16)
    lane = lax.broadcasted_iota(jnp.int32, (CHUNK, LANES), 1)
    for ci in range(tm // CHUNK):
        rows = slice(ci * CHUNK, (ci + 1) * CHUNK)
        g3 = jnp.concatenate([hi[rows], mid[rows], lo[rows]], axis=0)
        cum_f = _bdot(m_fwd, g3)
        cum_b = _bdot(m_bwd, g3)
        g_ref[rows, :] = jnp.where(lane < DN_HEADS, cum_f, jnp.where(lane < 2 * DN_HEADS, cum_b, beta[rows]))


def _zg_proj(h, w, alog, dtb, tm):
    n, t, d = h.shape
    cols = w.shape[1]
    return pl.pallas_call(
        functools.partial(_zg_kernel, tm=tm),
        out_shape=(jax.ShapeDtypeStruct((n, t, DN_DIM), BF16),
                   jax.ShapeDtypeStruct((n, t, LANES), F32)),
        grid=(n, t // tm),
        in_specs=[pl.BlockSpec((None, tm, d), lambda a, i: (a, i, 0)),
                  pl.BlockSpec((d, cols), lambda a, i: (0, 0)),
                  pl.BlockSpec((1, LANES), lambda a, i: (0, 0)),
                  pl.BlockSpec((1, LANES), lambda a, i: (0, 0))],
        out_specs=(pl.BlockSpec((None, tm, DN_DIM), lambda a, i: (a, i, 0)),
                   pl.BlockSpec((None, tm, LANES), lambda a, i: (a, i, 0))),
        compiler_params=_params(("parallel", "parallel")),
        name="zg_proj",
    )(h, w, alog, dtb)


def _sc_kernel(h_ref, hp_ref, hn_ref, w_ref, cw_ref, gain_ref, o_ref, hx_ref, *, tm, groups):
    i = pl.program_id(1)
    j = pl.program_id(2)

    @pl.when(j == 0)
    def _():
        _fill_halo(hx_ref, h_ref, hp_ref, hn_ref, tm, i == 0, i == pl.num_programs(1) - 1)

    p = _bdot(hx_ref[...], w_ref[...])
    gd = SC_GROUP_DIM
    for gi in range(groups):
        base = gi * 3 * gd
        b = p[HALO:HALO + tm, base:base + gd]
        cx = p[:, base + gd:base + 2 * gd] * p[:, base + 2 * gd:base + 3 * gd]
        sl = slice(gi * gd, (gi + 1) * gd)
        y = b * _conv3_rows(cx, cw_ref[:, sl], tm)
        o_ref[:, sl] = (_rms(y) * gain_ref[:, sl]).astype(o_ref.dtype)


def _sc_proj(h, w, cw, gain, tm, groups=2):
    n, t, d = h.shape
    main, prev, nxt = _halo_specs(tm, t, d)
    tn = groups * 3 * SC_GROUP_DIM
    to = groups * SC_GROUP_DIM
    return pl.pallas_call(
        functools.partial(_sc_kernel, tm=tm, groups=groups),
        out_shape=jax.ShapeDtypeStruct((n, t, SC_DIM), BF16),
        grid=(n, t // tm, SC_GROUPS // groups),
        in_specs=[main, prev, nxt,
                  pl.BlockSpec((d, tn), lambda a, i, j: (0, j)),
                  pl.BlockSpec((3, to), lambda a, i, j: (0, j)),
                  pl.BlockSpec((1, to), lambda a, i, j: (0, j))],
        out_specs=pl.BlockSpec((None, tm, to), lambda a, i, j: (a, i, j)),
        scratch_shapes=[pltpu.VMEM((tm + 2 * HALO, d), BF16)],
        compiler_params=_params(("parallel", "parallel", "arbitrary")),
        name="sc_proj",
    )(h, h, h, w, cw, gain.reshape(1, SC_DIM))


def _delta_kernel(qf_ref, kf_ref, vf_ref, gcf_ref, grf_ref, qb_ref, kb_ref, vb_ref, gcb_ref, grb_ref,
                  of_ref, ob_ref, s_ref, *, tb):
    @pl.when(pl.program_id(1) == 0)
    def _():
        s_ref[...] = jnp.zeros_like(s_ref)

    nch = tb // CHUNK
    r = lax.broadcasted_iota(jnp.int32, (CHUNK, CHUNK), 0)
    c = lax.broadcasted_iota(jnp.int32, (CHUNK, CHUNK), 1)
    eye = (r == c).astype(F32)
    masks = ((r >= c, r > c), (r <= c, r < c))
    refs = ((qf_ref, kf_ref, vf_ref, gcf_ref, grf_ref, of_ref),
            (qb_ref, kb_ref, vb_ref, gcb_ref, grb_ref, ob_ref))

    def chunk_body(jj, carry):
        chains = []
        for d, (q_ref, k_ref, v_ref, gc_ref, gr_ref, o_ref) in enumerate(refs):
            cc = (nch - 1 - jj) if d else jj
            rows = pl.ds(pl.multiple_of(cc * CHUNK, CHUNK), CHUNK)
            gcol = gc_ref[rows, :]
            grow = gr_ref[cc]
            goff = d * DN_HEADS
            boff = 2 * DN_HEADS + goff
            cum_c = gcol[:, goff:goff + DN_HEADS]
            b_c = gcol[:, boff:boff + DN_HEADS]
            cum_r = grow[goff:goff + DN_HEADS, :]
            b_r = grow[boff:boff + DN_HEADS, :]
            last = 0 if d else CHUNK - 1
            gl_c = cum_c[last:last + 1, :]
            eg_c = jnp.exp(cum_c) * Q_SCALE
            ekd_c = jnp.exp(gl_c - cum_c)
            egl = jnp.exp(gl_c)
            ebr = b_r * jnp.exp(cum_r)
            for hh in range(DN_HEADS):
                sl = slice(hh * DN_HEAD_DIM, (hh + 1) * DN_HEAD_DIM)
                chains.append(dict(
                    d=d, hh=hh, rows=rows, sl=sl, q_ref=q_ref, k_ref=k_ref, v_ref=v_ref, o_ref=o_ref,
                    mle=masks[d][0], strict=masks[d][1],
                    gc=cum_c[:, hh:hh + 1], gr=cum_r[hh:hh + 1, :], bc=b_c[:, hh:hh + 1], br=b_r[hh:hh + 1, :],
                    egc=eg_c[:, hh:hh + 1], ekd=ekd_c[:, hh:hh + 1], egl=egl[:, hh:hh + 1], ebr=ebr[hh:hh + 1, :]))

        def ld(ch, name):
            return ch[name + "_ref"][ch["rows"], ch["sl"]]

        decay = [jnp.where(ch["mle"], jnp.exp(jnp.where(ch["mle"], ch["gc"] - ch["gr"], 0.0)), 0.0)
                 for ch in chains]
        kq = [lax.dot_general(jnp.concatenate([ld(ch, "k"), ld(ch, "q")], axis=0), ld(ch, "k"), NT_DIMS,
                              preferred_element_type=F32) for ch in chains]
        a_qk = [(x[CHUNK:] * dc * Q_SCALE).astype(BF16) for x, dc in zip(kq, decay)]
        pw = [jnp.where(ch["strict"], -(x[:CHUNK] * dc) * ch["bc"], 0.0) for ch, x, dc in zip(chains, kq, decay)]
        tinv = [eye + p for p in pw]
        pw = [_bdot(p.astype(BF16), p.astype(BF16)) for p in pw]
        for _ in range(4):
            both = [_bdot(jnp.concatenate([t, p], axis=0).astype(BF16), p.astype(BF16)) for t, p in zip(tinv, pw)]
            tinv = [t + b[:CHUNK] for t, b in zip(tinv, both)]
            pw = [b[CHUNK:] for b in both]
        tinv = [t + _bdot(t.astype(BF16), p.astype(BF16)) for t, p in zip(tinv, pw)]
        u = [_bdot((t * ch["br"]).astype(BF16), ld(ch, "v")) for ch, t in zip(chains, tinv)]
        w = [_bdot((t * ch["ebr"]).astype(BF16), ld(ch, "k")) for ch, t in zip(chains, tinv)]
        qg = [(ld(ch, "q").astype(F32) * ch["egc"]).astype(BF16) for ch in chains]
        kdec = [(ld(ch, "k").astype(F32) * ch["ekd"]).astype(BF16) for ch in chains]
        s = [s_ref[ch["d"], ch["hh"]] for ch in chains]
        wq = [_bdot(jnp.concatenate([x.astype(BF16), y], axis=0), z.astype(BF16)) for x, y, z in zip(w, qg, s)]
        vn = [(x - y[:CHUNK]).astype(BF16) for x, y in zip(u, wq)]
        for ch, y, a, x in zip(chains, wq, a_qk, vn):
            ch["o_ref"][ch["rows"], ch["sl"]] = y[CHUNK:] + _bdot(a, x)
        for ch, z, kd, x in zip(chains, s, kdec, vn):
            s_ref[ch["d"], ch["hh"]] = z * ch["egl"] + lax.dot_general(kd, x, TN_DIMS, preferred_element_type=F32)
        return carry

    lax.fori_loop(0, nch, chunk_body, 0)


def _delta_scan(qkv, gates, gates_t, tb):
    n, t, _ = qkv.shape
    nb = t // tb
    nch = tb // CHUNK
    fwd = lambda i: i
    bwd = lambda i: nb - 1 - i

    def specs(blk):
        return [pl.BlockSpec((None, tb, DN_DIM), lambda a, i: (a, blk(i), 0)),
                pl.BlockSpec((None, tb, DN_DIM), lambda a, i: (a, blk(i), 1)),
                pl.BlockSpec((None, tb, DN_DIM), lambda a, i: (a, blk(i), 2)),
                pl.BlockSpec((None, tb, LANES), lambda a, i: (a, blk(i), 0)),
                pl.BlockSpec((None, nch, N_GATE_COLS, CHUNK), lambda a, i: (a, blk(i), 0, 0))]

    out = jax.ShapeDtypeStruct((n, t, DN_DIM), F32)
    args = (qkv, qkv, qkv, gates, gates_t)
    return pl.pallas_call(
        functools.partial(_delta_kernel, tb=tb),
        out_shape=(out, out),
        grid=(n, nb),
        in_specs=specs(fwd) + specs(bwd),
        out_specs=(pl.BlockSpec((None, tb, DN_DIM), lambda a, i: (a, fwd(i), 0)),
                   pl.BlockSpec((None, tb, DN_DIM), lambda a, i: (a, bwd(i), 0))),
        scratch_shapes=[pltpu.VMEM((2, DN_HEADS, DN_HEAD_DIM, DN_HEAD_DIM), F32)],
        compiler_params=_params(("parallel", "arbitrary")),
        name="delta_scan",
    )(*args, *args)


def _out_kernel(of_ref, ob_ref, zs_ref, ysc_ref, x_ref, w_ref, dnw_ref, npost_ref, nnext_ref,
                xo_ref, hn_ref, odn_ref):
    for hh in range(DN_HEADS):
        sl = slice(hh * DN_HEAD_DIM, (hh + 1) * DN_HEAD_DIM)
        o = of_ref[:, sl] + ob_ref[:, sl]
        y = _rms(o) * dnw_ref[...] * zs_ref[:, sl].astype(F32)
        odn_ref[:, sl] = y.astype(odn_ref.dtype)
    m = _bdot(odn_ref[...], w_ref[0:DN_DIM, :]) + _bdot(ysc_ref[...], w_ref[DN_DIM:, :])
    xn = x_ref[...] + _rms(m) * npost_ref[...]
    xo_ref[...] = xn
    hn_ref[...] = (_rms(xn) * nnext_ref[...]).astype(hn_ref.dtype)


def _out_proj(o_f, o_b, zs, ysc, x, w, dn_w, n_post, n_next, tm):
    n, t, d = x.shape
    row = lambda width: pl.BlockSpec((None, tm, width), lambda a, i: (a, i, 0))
    vec = lambda width: pl.BlockSpec((1, width), lambda a, i: (0, 0))
    return pl.pallas_call(
        _out_kernel,
        out_shape=(jax.ShapeDtypeStruct((n, t, d), F32), jax.ShapeDtypeStruct((n, t, d), BF16)),
        grid=(n, t // tm),
        in_specs=[row(DN_DIM), row(DN_DIM), row(DN_DIM), row(SC_DIM), row(d),
                  pl.BlockSpec((d, d), lambda a, i: (0, 0)),
                  vec(DN_HEAD_DIM), vec(d), vec(d)],
        out_specs=(row(d), row(d)),
        scratch_shapes=[pltpu.VMEM((tm, DN_DIM), BF16)],
        compiler_params=_params(("parallel", "parallel")),
        name="out_proj",
    )(o_f, o_b, zs, ysc, x, w, dn_w.reshape(1, -1), n_post.reshape(1, d), n_next.reshape(1, d))


def _ffn_kernel(h_ref, hp_ref, hn_ref, wa_ref, wb_ref, cw_ref, wd_ref, x_ref, npost_ref, nnext_ref,
                xo_ref, ho_ref, hx_ref, acc_ref, *, tm):
    i = pl.program_id(1)
    f = pl.program_id(2)

    @pl.when(f == 0)
    def _():
        _fill_halo(hx_ref, h_ref, hp_ref, hn_ref, tm, i == 0, i == pl.num_programs(1) - 1)
        acc_ref[...] = jnp.zeros_like(acc_ref)

    a = _bdot(hx_ref[...], wa_ref[...])
    b = _bdot(h_ref[...], wb_ref[...])
    act = (_silu(_conv3_rows(a, cw_ref[...], tm)) * b).astype(BF16)
    acc_ref[...] += _bdot(act, wd_ref[...])

    @pl.when(f == pl.num_programs(2) - 1)
    def _():
        xn = x_ref[...] + _rms(acc_ref[...]) * npost_ref[...]
        xo_ref[...] = xn
        ho_ref[...] = (_rms(xn) * nnext_ref[...]).astype(ho_ref.dtype)


def _ffn(h, x, w_up, cw, w_down, n_post, n_next, tm, tf=512):
    n, t, d = x.shape
    nf = D_FF // tf
    main, prev, nxt = _halo_specs(tm, t, d)
    row = lambda: pl.BlockSpec((None, tm, d), lambda a, i, f: (a, i, 0))
    vec = lambda: pl.BlockSpec((1, d), lambda a, i, f: (0, 0))
    return pl.pallas_call(
        functools.partial(_ffn_kernel, tm=tm),
        out_shape=(jax.ShapeDtypeStruct((n, t, d), F32), jax.ShapeDtypeStruct((n, t, d), BF16)),
        grid=(n, t // tm, nf),
        in_specs=[main, prev, nxt,
                  pl.BlockSpec((d, tf), lambda a, i, f: (0, f)),
                  pl.BlockSpec((d, tf), lambda a, i, f: (0, f + nf)),
                  pl.BlockSpec((3, tf), lambda a, i, f: (0, f)),
                  pl.BlockSpec((tf, d), lambda a, i, f: (f, 0)),
                  row(), vec(), vec()],
        out_specs=(row(), row()),
        scratch_shapes=[pltpu.VMEM((tm + 2 * HALO, d), BF16), pltpu.VMEM((tm, d), F32)],
        compiler_params=_params(("parallel", "parallel", "arbitrary")),
        name="ffn",
    )(h, h, h, w_up, w_up, cw, w_down, x, n_post.reshape(1, d), n_next.reshape(1, d))


def _tiles(t):
    return min(512, t), min(256, t)


def _prepare(w_in, a_log, dt_bias):
    depth = w_in.shape[0]
    w_qkv = w_in[:, :, :3 * DN_DIM].astype(BF16)
    z_end = 4 * DN_DIM
    g_end = z_end + N_GATE_COLS
    pad = jnp.zeros((depth, D_MODEL, LANES - N_GATE_COLS), w_in.dtype)
    w_zg = jnp.concatenate([w_in[:, :, 3 * DN_DIM:g_end], pad], axis=-1).astype(BF16)
    w_sc = w_in[:, :, g_end:].reshape(depth, D_MODEL, 3, SC_GROUPS, SC_GROUP_DIM)
    w_sc = w_sc.transpose(0, 1, 3, 2, 4).reshape(depth, D_MODEL, 3 * SC_DIM).astype(BF16)
    vpad = jnp.zeros((depth, 1, LANES - 2 * DN_HEADS), F32)
    alog = jnp.concatenate([a_log.reshape(depth, 1, 2 * DN_HEADS).astype(F32), vpad], axis=-1)
    dtb = jnp.concatenate([dt_bias.reshape(depth, 1, 2 * DN_HEADS).astype(F32), vpad], axis=-1)
    return w_qkv, w_zg, w_sc, alog, dtb


def _trunk(x, prm):
    (norm_mix_pre, w_qkv, w_zg, w_sc, alog, dtb, conv_qkv, dn_norm, conv_sc, sc_norm, w_out,
     norm_mix_post, norm_ffn_pre, w_up, conv_ffn, w_down, norm_ffn_post) = prm
    depth = w_qkv.shape[0]
    n, t, _ = x.shape
    tm, tb = _tiles(t)
    h = _prenorm(x, norm_mix_pre[0], tm)
    for l in range(depth):
        qkv = _qkv_proj(h, w_qkv[l], conv_qkv[l], tm)
        zs, gates = _zg_proj(h, w_zg[l], alog[l], dtb[l], tm)
        ysc = _sc_proj(h, w_sc[l], conv_sc[l], sc_norm[l], tm)
        gates_t = gates[:, :, :N_GATE_COLS].reshape(n, t // CHUNK, CHUNK, N_GATE_COLS).transpose(0, 1, 3, 2)
        o_f, o_b = _delta_scan(qkv, gates, gates_t, tb)
        x, h = _out_proj(o_f, o_b, zs, ysc, x, w_out[l], dn_norm[l], norm_mix_post[l], norm_ffn_pre[l], tm)
        n_next = norm_mix_pre[(l + 1) % depth]
        x, h = _ffn(h, x, w_up[l], conv_ffn[l], w_down[l], norm_ffn_post[l], n_next, tm)
    return x


def kernel(x_prompt, x_sample, norm_mix_pre, w_in, conv_qkv, a_log, dt_bias, dn_norm, conv_sc, sc_norm, w_out, norm_mix_post, norm_ffn_pre, w_up, conv_ffn, w_down, norm_ffn_post):
    w_qkv, w_zg, w_sc, alog, dtb = _prepare(w_in, a_log, dt_bias)
    prm = (norm_mix_pre, w_qkv, w_zg, w_sc, alog, dtb, conv_qkv, dn_norm, conv_sc, sc_norm,
           w_out.astype(BF16), norm_mix_post, norm_ffn_pre, w_up.astype(BF16), conv_ffn,
           w_down.astype(BF16), norm_ffn_post)
    return (_trunk(x_prompt, prm), _trunk(x_sample, prm))
```

```python
import functools

import jax
import jax.numpy as jnp
from jax import lax
from jax.experimental import pallas as pl
from jax.experimental.pallas import tpu as pltpu

D_MODEL = 2048
DN_HEADS = 8
DN_HEAD_DIM = 128
DN_DIM = DN_HEADS * DN_HEAD_DIM
SC_GROUPS = 8
SC_GROUP_DIM = 128
SC_DIM = SC_GROUPS * SC_GROUP_DIM
D_FF = 5632
CHUNK = 64
N_GATE_COLS = 4 * DN_HEADS
NORM_EPS = 1e-6
L2_EPS = 1e-6
Q_SCALE = DN_HEAD_DIM ** -0.5

LANES = 128
HALO = 16
VMEM_LIMIT = 56 * 1024 * 1024

F32 = jnp.float32
BF16 = jnp.bfloat16
NT_DIMS = (((1,), (1,)), ((), ()))
TN_DIMS = (((0,), (0,)), ((), ()))


def _sigmoid(x):
    return 1.0 / (1.0 + jnp.exp(-x))


def _silu(x):
    return x * _sigmoid(x)


def _softplus(x):
    return jnp.maximum(x, 0.0) + jnp.log1p(jnp.exp(-jnp.abs(x)))


def _rms(x):
    return x * lax.rsqrt(jnp.mean(x * x, axis=-1, keepdims=True) + NORM_EPS)


def _bdot(a, b):
    return jnp.dot(a, b, preferred_element_type=F32)


def _conv3_rows(p, cw, tm):
    rows = p.shape[0]
    prev = pltpu.roll(p, 1, 0)[HALO:HALO + tm]
    nxt = pltpu.roll(p, rows - 1, 0)[HALO:HALO + tm]
    mid = p[HALO:HALO + tm]
    return prev * cw[0:1] + mid * cw[1:2] + nxt * cw[2:3]


def _fill_halo(hx_ref, h_ref, hp_ref, hn_ref, tm, first, last):
    hp = hp_ref[...]
    hn = hn_ref[...]
    hx_ref[0:HALO, :] = jnp.where(first, jnp.zeros_like(hp), hp)
    hx_ref[HALO:HALO + tm, :] = h_ref[...]
    hx_ref[HALO + tm:HALO + tm + HALO, :] = jnp.where(last, jnp.zeros_like(hn), hn)


def _with_halo(h_ref, hp_ref, hn_ref, first, last):
    hp = hp_ref[...]
    hn = hn_ref[...]
    return jnp.concatenate([jnp.where(first, jnp.zeros_like(hp), hp), h_ref[...],
                            jnp.where(last, jnp.zeros_like(hn), hn)], axis=0)


def _halo_specs(tm, t_len, width):
    per = tm // HALO
    nblk = t_len // HALO
    main = pl.BlockSpec((None, tm, width), lambda n, i, *_: (n, i, 0))
    prev = pl.BlockSpec((None, HALO, width), lambda n, i, *_: (n, jnp.maximum(i * per - 1, 0), 0))
    nxt = pl.BlockSpec((None, HALO, width), lambda n, i, *_: (n, jnp.minimum((i + 1) * per, nblk - 1), 0))
    return main, prev, nxt


def _params(sem):
    return pltpu.CompilerParams(dimension_semantics=sem, vmem_limit_bytes=VMEM_LIMIT)


def _prenorm_kernel(x_ref, w_ref, o_ref):
    o_ref[...] = (_rms(x_ref[...]) * w_ref[...]).astype(o_ref.dtype)


def _prenorm(x, w, tm):
    n, t, d = x.shape
    return pl.pallas_call(
        _prenorm_kernel,
        out_shape=jax.ShapeDtypeStruct((n, t, d), BF16),
        grid=(n, t // tm),
        in_specs=[pl.BlockSpec((None, tm, d), lambda a, i: (a, i, 0)),
                  pl.BlockSpec((1, d), lambda a, i: (0, 0))],
        out_specs=pl.BlockSpec((None, tm, d), lambda a, i: (a, i, 0)),
        compiler_params=_params(("parallel", "parallel")),
        name="prenorm",
    )(x, w.reshape(1, d))


QKV_TILE = 512
SC_TILE_GROUPS = 2


def _pipelined(tasks):
    pending = None
    for matmul, epilogue in tasks:
        val = matmul()
        if pending is not None:
            pending[0](pending[1])
        pending = (epilogue, val)
    pending[0](pending[1])


def _split3(x):
    hi = x.astype(BF16)
    r1 = x - hi.astype(F32)
    mid = r1.astype(BF16)
    lo = (r1 - mid.astype(F32)).astype(BF16)
    return hi, mid, lo


def _inproj_kernel(h_ref, hp_ref, hn_ref, wq_ref, wz_ref, ws_ref, cwq_ref, cws_ref, gain_ref, alog_ref, dtb_ref,
                   qkv_ref, z_ref, g_ref, ysc_ref, *, tm):
    i = pl.program_id(1)
    hx = _with_halo(h_ref, hp_ref, hn_ref, i == 0, i == pl.num_programs(1) - 1)
    h = h_ref[...]
    gd = SC_GROUP_DIM
    sc_tile = SC_TILE_GROUPS * 3 * gd

    def halo_matmul(w_ref, c0, width):
        return _bdot(hx, w_ref[:, c0:c0 + width])

    def plain_matmul(w_ref, c0, width):
        return _bdot(h, w_ref[:, c0:c0 + width])

    def gate_epilogue(gl):
        g = -jnp.exp(alog_ref[...]) * _softplus(gl + dtb_ref[...])
        beta = _sigmoid(gl)
        hi, mid, lo = _split3(g)
        r = lax.broadcasted_iota(jnp.int32, (CHUNK, 3 * CHUNK), 0)
        c = lax.broadcasted_iota(jnp.int32, (CHUNK, 3 * CHUNK), 1) % CHUNK
        m_fwd = (c <= r).astype(BF16)
        m_bwd = (c >= r).astype(BF16)
        lane = lax.broadcasted_iota(jnp.int32, (CHUNK, LANES), 1)
        for ci in range(tm // CHUNK):
            rows = slice(ci * CHUNK, (ci + 1) * CHUNK)
            g3 = jnp.concatenate([hi[rows], mid[rows], lo[rows]], axis=0)
            cum_f = _bdot(m_fwd, g3)
            cum_b = _bdot(m_bwd, g3)
            g_ref[rows, :] = jnp.where(lane < DN_HEADS, cum_f,
                                       jnp.where(lane < 2 * DN_HEADS, cum_b, beta[rows]))

    def qkv_epilogue(c0, p):
        y = _silu(_conv3_rows(p, cwq_ref[:, c0:c0 + QKV_TILE], tm))
        if c0 >= 2 * DN_DIM:
            qkv_ref[:, c0:c0 + QKV_TILE] = y.astype(qkv_ref.dtype)
            return
        for hh in range(QKV_TILE // DN_HEAD_DIM):
            yh = y[:, hh * DN_HEAD_DIM:(hh + 1) * DN_HEAD_DIM]
            ss = jnp.sum(yh * yh, axis=-1, keepdims=True)
            sl = slice(c0 + hh * DN_HEAD_DIM, c0 + (hh + 1) * DN_HEAD_DIM)
            qkv_ref[:, sl] = (yh * lax.rsqrt(ss + L2_EPS)).astype(qkv_ref.dtype)

    def z_epilogue(c0, p):
        z_ref[:, c0:c0 + QKV_TILE] = _silu(p).astype(z_ref.dtype)

    def sc_epilogue(jn, p):
        for gi in range(SC_TILE_GROUPS):
            base = gi * 3 * gd
            b = p[HALO:HALO + tm, base:base + gd]
            cx = p[:, base + gd:base + 2 * gd] * p[:, base + 2 * gd:base + 3 * gd]
            g0 = (jn * SC_TILE_GROUPS + gi) * gd
            y = b * _conv3_rows(cx, cws_ref[:, g0:g0 + gd], tm)
            ysc_ref[:, g0:g0 + gd] = (_rms(y) * gain_ref[:, g0:g0 + gd]).astype(ysc_ref.dtype)

    part = functools.partial
    tasks = [(part(plain_matmul, wz_ref, DN_DIM, LANES), gate_epilogue)]
    tasks += [(part(halo_matmul, wq_ref, c0, QKV_TILE), part(qkv_epilogue, c0))
              for c0 in range(0, 3 * DN_DIM, QKV_TILE)]
    tasks += [(part(plain_matmul, wz_ref, c0, QKV_TILE), part(z_epilogue, c0))
              for c0 in range(0, DN_DIM, QKV_TILE)]
    tasks += [(part(halo_matmul, ws_ref, jn * sc_tile, sc_tile), part(sc_epilogue, jn))
              for jn in range(SC_GROUPS // SC_TILE_GROUPS)]
    _pipelined(tasks)


def _in_proj(h, w_qkv, w_zg, w_sc, cw_qkv, cw_sc, gain, alog, dtb, tm):
    n, t, d = h.shape
    main, prev, nxt = _halo_specs(tm, t, d)
    whole = lambda shape: pl.BlockSpec(shape, lambda a, i: (0, 0))
    row = lambda width: pl.BlockSpec((None, tm, width), lambda a, i: (a, i, 0))
    return pl.pallas_call(
        functools.partial(_inproj_kernel, tm=tm),
        out_shape=(jax.ShapeDtypeStruct((n, t, 3 * DN_DIM), BF16),
                   jax.ShapeDtypeStruct((n, t, DN_DIM), BF16),
                   jax.ShapeDtypeStruct((n, t, LANES), F32),
                   jax.ShapeDtypeStruct((n, t, SC_DIM), BF16)),
        grid=(n, t // tm),
        in_specs=[main, prev, nxt, whole(w_qkv.shape), whole(w_zg.shape), whole(w_sc.shape),
                  whole((3, 3 * DN_DIM)), whole((3, SC_DIM)), whole((1, SC_DIM)),
                  whole((1, LANES)), whole((1, LANES))],
        out_specs=(row(3 * DN_DIM), row(DN_DIM), row(LANES), row(SC_DIM)),
        compiler_params=_params(("parallel", "parallel")),
        name="in_proj",
    )(h, h, h, w_qkv, w_zg, w_sc, cw_qkv, cw_sc, gain.reshape(1, SC_DIM), alog, dtb)


def _delta_kernel(qf_ref, kf_ref, vf_ref, gcf_ref, grf_ref, qb_ref, kb_ref, vb_ref, gcb_ref, grb_ref,
                  of_ref, ob_ref, s_ref, *, tb):
    @pl.when(pl.program_id(1) == 0)
    def _():
        s_ref[...] = jnp.zeros_like(s_ref)

    nch = tb // CHUNK
    r = lax.broadcasted_iota(jnp.int32, (CHUNK, CHUNK), 0)
    c = lax.broadcasted_iota(jnp.int32, (CHUNK, CHUNK), 1)
    eye = (r == c).astype(F32)
    masks = ((r >= c, r > c), (r <= c, r < c))
    refs = ((qf_ref, kf_ref, vf_ref, gcf_ref, grf_ref, of_ref),
            (qb_ref, kb_ref, vb_ref, gcb_ref, grb_ref, ob_ref))

    def chunk_body(jj, carry):
        chains = []
        for d, (q_ref, k_ref, v_ref, gc_ref, gr_ref, o_ref) in enumerate(refs):
            cc = (nch - 1 - jj) if d else jj
            rows = pl.ds(pl.multiple_of(cc * CHUNK, CHUNK), CHUNK)
            gcol = gc_ref[rows, :]
            grow = gr_ref[cc]
            goff = d * DN_HEADS
            boff = 2 * DN_HEADS + goff
            cum_c = gcol[:, goff:goff + DN_HEADS]
            b_c = gcol[:, boff:boff + DN_HEADS]
            cum_r = grow[goff:goff + DN_HEADS, :]
            b_r = grow[boff:boff + DN_HEADS, :]
            last = 0 if d else CHUNK - 1
            gl_c = cum_c[last:last + 1, :]
            eg_c = jnp.exp(cum_c) * Q_SCALE
            ekd_c = jnp.exp(gl_c - cum_c)
            egl = jnp.exp(gl_c)
            ebr = b_r * jnp.exp(cum_r)
            for hh in range(DN_HEADS):
                sl = slice(hh * DN_HEAD_DIM, (hh + 1) * DN_HEAD_DIM)
                chains.append(dict(
                    d=d, hh=hh, rows=rows, sl=sl, q_ref=q_ref, k_ref=k_ref, v_ref=v_ref, o_ref=o_ref,
                    mle=masks[d][0], strict=masks[d][1],
                    gc=cum_c[:, hh:hh + 1], gr=cum_r[hh:hh + 1, :], bc=b_c[:, hh:hh + 1], br=b_r[hh:hh + 1, :],
                    egc=eg_c[:, hh:hh + 1], ekd=ekd_c[:, hh:hh + 1], egl=egl[:, hh:hh + 1], ebr=ebr[hh:hh + 1, :]))

        def ld(ch, name):
            return ch[name + "_ref"][ch["rows"], ch["sl"]]

        decay = [jnp.where(ch["mle"], jnp.exp(jnp.where(ch["mle"], ch["gc"] - ch["gr"], 0.0)), 0.0)
                 for ch in chains]
        kq = [lax.dot_general(jnp.concatenate([ld(ch, "k"), ld(ch, "q")], axis=0), ld(ch, "k"), NT_DIMS,
                              preferred_element_type=F32) for ch in chains]
        a_qk = [(x[CHUNK:] * dc * Q_SCALE).astype(BF16) for x, dc in zip(kq, decay)]
        pw = [jnp.where(ch["strict"], -(x[:CHUNK] * dc) * ch["bc"], 0.0) for ch, x, dc in zip(chains, kq, decay)]
        tinv = [eye + p for p in pw]
        pw = [_bdot(p.astype(BF16), p.astype(BF16)) for p in pw]
        for _ in range(4):
            both = [_bdot(jnp.concatenate([t, p], axis=0).astype(BF16), p.astype(BF16)) for t, p in zip(tinv, pw)]
            tinv = [t + b[:CHUNK] for t, b in zip(tinv, both)]
            pw = [b[CHUNK:] for b in both]
        tinv = [t + _bdot(t.astype(BF16), p.astype(BF16)) for t, p in zip(tinv, pw)]
        u = [_bdot((t * ch["br"]).astype(BF16), ld(ch, "v")) for ch, t in zip(chains, tinv)]
        w = [_bdot((t * ch["ebr"]).astype(BF16), ld(ch, "k")) for ch, t in zip(chains, tinv)]
        qg = [(ld(ch, "q").astype(F32) * ch["egc"]).astype(BF16) for ch in chains]
        kdec = [(ld(ch, "k").astype(F32) * ch["ekd"]).astype(BF16) for ch in chains]
        s = [s_ref[ch["d"], ch["hh"]] for ch in chains]
        wq = [_bdot(jnp.concatenate([x.astype(BF16), y], axis=0), z.astype(BF16)) for x, y, z in zip(w, qg, s)]
        vn = [(x - y[:CHUNK]).astype(BF16) for x, y in zip(u, wq)]
        for ch, y, a, x in zip(chains, wq, a_qk, vn):
            ch["o_ref"][ch["rows"], ch["sl"]] = y[CHUNK:] + _bdot(a, x)
        for ch, z, kd, x in zip(chains, s, kdec, vn):
            s_ref[ch["d"], ch["hh"]] = z * ch["egl"] + lax.dot_general(kd, x, TN_DIMS, preferred_element_type=F32)
        return carry

    lax.fori_loop(0, nch, chunk_body, 0)


def _delta_scan(qkv, gates, gates_t, tb):
    n, t, _ = qkv.shape
    nb = t // tb
    nch = tb // CHUNK
    fwd = lambda i: i
    bwd = lambda i: nb - 1 - i

    def specs(blk):
        return [pl.BlockSpec((None, tb, DN_DIM), lambda a, i: (a, blk(i), 0)),
                pl.BlockSpec((None, tb, DN_DIM), lambda a, i: (a, blk(i), 1)),
                pl.BlockSpec((None, tb, DN_DIM), lambda a, i: (a, blk(i), 2)),
                pl.BlockSpec((None, tb, LANES), lambda a, i: (a, blk(i), 0)),
                pl.BlockSpec((None, nch, N_GATE_COLS, CHUNK), lambda a, i: (a, blk(i), 0, 0))]

    out = jax.ShapeDtypeStruct((n, t, DN_DIM), F32)
    args = (qkv, qkv, qkv, gates, gates_t)
    return pl.pallas_call(
        functools.partial(_delta_kernel, tb=tb),
        out_shape=(out, out),
        grid=(n, nb),
        in_specs=specs(fwd) + specs(bwd),
        out_specs=(pl.BlockSpec((None, tb, DN_DIM), lambda a, i: (a, fwd(i), 0)),
                   pl.BlockSpec((None, tb, DN_DIM), lambda a, i: (a, bwd(i), 0))),
        scratch_shapes=[pltpu.VMEM((2, DN_HEADS, DN_HEAD_DIM, DN_HEAD_DIM), F32)],
        compiler_params=_params(("parallel", "arbitrary")),
        name="delta_scan",
    )(*args, *args)


def _out_kernel(of_ref, ob_ref, zs_ref, ysc_ref, x_ref, w_ref, dnw_ref, npost_ref, nnext_ref,
                xo_ref, hn_ref, odn_ref):
    for hh in range(DN_HEADS):
        sl = slice(hh * DN_HEAD_DIM, (hh + 1) * DN_HEAD_DIM)
        o = of_ref[:, sl] + ob_ref[:, sl]
        y = _rms(o) * dnw_ref[...] * zs_ref[:, sl].astype(F32)
        odn_ref[:, sl] = y.astype(odn_ref.dtype)
    m = _bdot(odn_ref[...], w_ref[0:DN_DIM, :]) + _bdot(ysc_ref[...], w_ref[DN_DIM:, :])
    xn = x_ref[...] + _rms(m) * npost_ref[...]
    xo_ref[...] = xn
    hn_ref[...] = (_rms(xn) * nnext_ref[...]).astype(hn_ref.dtype)


def _out_proj(o_f, o_b, zs, ysc, x, w, dn_w, n_post, n_next, tm):
    n, t, d = x.shape
    row = lambda width: pl.BlockSpec((None, tm, width), lambda a, i: (a, i, 0))
    vec = lambda width: pl.BlockSpec((1, width), lambda a, i: (0, 0))
    return pl.pallas_call(
        _out_kernel,
        out_shape=(jax.ShapeDtypeStruct((n, t, d), F32), jax.ShapeDtypeStruct((n, t, d), BF16)),
        grid=(n, t // tm),
        in_specs=[row(DN_DIM), row(DN_DIM), row(DN_DIM), row(SC_DIM), row(d),
                  pl.BlockSpec((d, d), lambda a, i: (0, 0)),
                  vec(DN_HEAD_DIM), vec(d), vec(d)],
        out_specs=(row(d), row(d)),
        scratch_shapes=[pltpu.VMEM((tm, DN_DIM), BF16)],
        compiler_params=_params(("parallel", "parallel")),
        name="out_proj",
    )(o_f, o_b, zs, ysc, x, w, dn_w.reshape(1, -1), n_post.reshape(1, d), n_next.reshape(1, d))


def _ffn_kernel(h_ref, hp_ref, hn_ref, wa_ref, wb_ref, cw_ref, wd_ref, x_ref, npost_ref, nnext_ref,
                xo_ref, ho_ref, hx_ref, acc_ref, *, tm):
    i = pl.program_id(1)
    f = pl.program_id(2)

    @pl.when(f == 0)
    def _():
        _fill_halo(hx_ref, h_ref, hp_ref, hn_ref, tm, i == 0, i == pl.num_programs(1) - 1)
        acc_ref[...] = jnp.zeros_like(acc_ref)

    a = _bdot(hx_ref[...], wa_ref[...])
    b = _bdot(h_ref[...], wb_ref[...])
    act = (_silu(_conv3_rows(a, cw_ref[...], tm)) * b).astype(BF16)
    acc_ref[...] += _bdot(act, wd_ref[...])

    @pl.when(f == pl.num_programs(2) - 1)
    def _():
        xn = x_ref[...] + _rms(acc_ref[...]) * npost_ref[...]
        xo_ref[...] = xn
        ho_ref[...] = (_rms(xn) * nnext_ref[...]).astype(ho_ref.dtype)


def _ffn(h, x, w_up, cw, w_down, n_post, n_next, tm, tf=512):
    n, t, d = x.shape
    nf = D_FF // tf
    main, prev, nxt = _halo_specs(tm, t, d)
    row = lambda: pl.BlockSpec((None, tm, d), lambda a, i, f: (a, i, 0))
    vec = lambda: pl.BlockSpec((1, d), lambda a, i, f: (0, 0))
    return pl.pallas_call(
        functools.partial(_ffn_kernel, tm=tm),
        out_shape=(jax.ShapeDtypeStruct((n, t, d), F32), jax.ShapeDtypeStruct((n, t, d), BF16)),
        grid=(n, t // tm, nf),
        in_specs=[main, prev, nxt,
                  pl.BlockSpec((d, tf), lambda a, i, f: (0, f)),
                  pl.BlockSpec((d, tf), lambda a, i, f: (0, f + nf)),
                  pl.BlockSpec((3, tf), lambda a, i, f: (0, f)),
                  pl.BlockSpec((tf, d), lambda a, i, f: (f, 0)),
                  row(), vec(), vec()],
        out_specs=(row(), row()),
        scratch_shapes=[pltpu.VMEM((tm + 2 * HALO, d), BF16), pltpu.VMEM((tm, d), F32)],
        compiler_params=_params(("parallel", "parallel", "arbitrary")),
        name="ffn",
    )(h, h, h, w_up, w_up, cw, w_down, x, n_post.reshape(1, d), n_next.reshape(1, d))


def _tiles(t):
    return min(512, t), min(256, t)


def _prepare(w_in, a_log, dt_bias):
    depth = w_in.shape[0]
    w_qkv = w_in[:, :, :3 * DN_DIM].astype(BF16)
    z_end = 4 * DN_DIM
    g_end = z_end + N_GATE_COLS
    pad = jnp.zeros((depth, D_MODEL, LANES - N_GATE_COLS), w_in.dtype)
    w_zg = jnp.concatenate([w_in[:, :, 3 * DN_DIM:g_end], pad], axis=-1).astype(BF16)
    w_sc = w_in[:, :, g_end:].reshape(depth, D_MODEL, 3, SC_GROUPS, SC_GROUP_DIM)
    w_sc = w_sc.transpose(0, 1, 3, 2, 4).reshape(depth, D_MODEL, 3 * SC_DIM).astype(BF16)
    vpad = jnp.zeros((depth, 1, LANES - 2 * DN_HEADS), F32)
    alog = jnp.concatenate([a_log.reshape(depth, 1, 2 * DN_HEADS).astype(F32), vpad], axis=-1)
    dtb = jnp.concatenate([dt_bias.reshape(depth, 1, 2 * DN_HEADS).astype(F32), vpad], axis=-1)
    return w_qkv, w_zg, w_sc, alog, dtb


def _trunk(x, prm):
    (norm_mix_pre, w_qkv, w_zg, w_sc, alog, dtb, conv_qkv, dn_norm, conv_sc, sc_norm, w_out,
     norm_mix_post, norm_ffn_pre, w_up, conv_ffn, w_down, norm_ffn_post) = prm
    depth = w_qkv.shape[0]
    n, t, _ = x.shape
    tm, tb = _tiles(t)
    h = _prenorm(x, norm_mix_pre[0], tm)
    for l in range(depth):
        qkv, zs, gates, ysc = _in_proj(h, w_qkv[l], w_zg[l], w_sc[l], conv_qkv[l], conv_sc[l], sc_norm[l],
                                       alog[l], dtb[l], tm)
        gates_t = gates[:, :, :N_GATE_COLS].reshape(n, t // CHUNK, CHUNK, N_GATE_COLS).transpose(0, 1, 3, 2)
        o_f, o_b = _delta_scan(qkv, gates, gates_t, tb)
        x, h = _out_proj(o_f, o_b, zs, ysc, x, w_out[l], dn_norm[l], norm_mix_post[l], norm_ffn_pre[l], tm)
        n_next = norm_mix_pre[(l + 1) % depth]
        x, h = _ffn(h, x, w_up[l], conv_ffn[l], w_down[l], norm_ffn_post[l], n_next, tm)
    return x


def kernel(x_prompt, x_sample, norm_mix_pre, w_in, conv_qkv, a_log, dt_bias, dn_norm, conv_sc, sc_norm, w_out, norm_mix_post, norm_ffn_pre, w_up, conv_ffn, w_down, norm_ffn_post):
    w_qkv, w_zg, w_sc, alog, dtb = _prepare(w_in, a_log, dt_bias)
    prm = (norm_mix_pre, w_qkv, w_zg, w_sc, alog, dtb, conv_qkv, dn_norm, conv_sc, sc_norm,
           w_out.astype(BF16), norm_mix_post, norm_ffn_pre, w_up.astype(BF16), conv_ffn,
           w_down.astype(BF16), norm_ffn_post)
    return (_trunk(x_prompt, prm), _trunk(x_sample, prm))
```

```python
import functools

import jax
import jax.numpy as jnp
from jax import lax
from jax.experimental import pallas as pl
from jax.experimental.pallas import tpu as pltpu

D_MODEL = 2048
DN_HEADS = 8
DN_HEAD_DIM = 128
DN_DIM = DN_HEADS * DN_HEAD_DIM
SC_GROUPS = 8
SC_GROUP_DIM = 128
SC_DIM = SC_GROUPS * SC_GROUP_DIM
D_FF = 5632
CHUNK = 64
N_GATE_COLS = 4 * DN_HEADS
NORM_EPS = 1e-6
L2_EPS = 1e-6
Q_SCALE = DN_HEAD_DIM ** -0.5

LANES = 128
HALO = 16
VMEM_LIMIT = 56 * 1024 * 1024

F32 = jnp.float32
BF16 = jnp.bfloat16
NT_DIMS = (((1,), (1,)), ((), ()))
TN_DIMS = (((0,), (0,)), ((), ()))


def _sigmoid(x):
    return 1.0 / (1.0 + jnp.exp(-x))


def _silu(x):
    return x * _sigmoid(x)


def _softplus(x):
    return jnp.maximum(x, 0.0) + jnp.log1p(jnp.exp(-jnp.abs(x)))


def _rms(x):
    return x * lax.rsqrt(jnp.mean(x * x, axis=-1, keepdims=True) + NORM_EPS)


def _bdot(a, b):
    return jnp.dot(a, b, preferred_element_type=F32)


def _conv3_rows(p, cw, tm):
    rows = p.shape[0]
    prev = pltpu.roll(p, 1, 0)[HALO:HALO + tm]
    nxt = pltpu.roll(p, rows - 1, 0)[HALO:HALO + tm]
    mid = p[HALO:HALO + tm]
    return prev * cw[0:1] + mid * cw[1:2] + nxt * cw[2:3]


def _fill_halo(hx_ref, h_ref, hp_ref, hn_ref, tm, first, last):
    hp = hp_ref[...]
    hn = hn_ref[...]
    hx_ref[0:HALO, :] = jnp.where(first, jnp.zeros_like(hp), hp)
    hx_ref[HALO:HALO + tm, :] = h_ref[...]
    hx_ref[HALO + tm:HALO + tm + HALO, :] = jnp.where(last, jnp.zeros_like(hn), hn)


def _with_halo(h_ref, hp_ref, hn_ref, first, last):
    hp = hp_ref[...]
    hn = hn_ref[...]
    return jnp.concatenate([jnp.where(first, jnp.zeros_like(hp), hp), h_ref[...],
                            jnp.where(last, jnp.zeros_like(hn), hn)], axis=0)


def _halo_specs(tm, t_len, width):
    per = tm // HALO
    nblk = t_len // HALO
    main = pl.BlockSpec((None, tm, width), lambda n, i, *_: (n, i, 0))
    prev = pl.BlockSpec((None, HALO, width), lambda n, i, *_: (n, jnp.maximum(i * per - 1, 0), 0))
    nxt = pl.BlockSpec((None, HALO, width), lambda n, i, *_: (n, jnp.minimum((i + 1) * per, nblk - 1), 0))
    return main, prev, nxt


def _params(sem):
    return pltpu.CompilerParams(dimension_semantics=sem, vmem_limit_bytes=VMEM_LIMIT)


def _prenorm_kernel(x_ref, w_ref, o_ref):
    o_ref[...] = (_rms(x_ref[...]) * w_ref[...]).astype(o_ref.dtype)


def _prenorm(x, w, tm):
    n, t, d = x.shape
    return pl.pallas_call(
        _prenorm_kernel,
        out_shape=jax.ShapeDtypeStruct((n, t, d), BF16),
        grid=(n, t // tm),
        in_specs=[pl.BlockSpec((None, tm, d), lambda a, i: (a, i, 0)),
                  pl.BlockSpec((1, d), lambda a, i: (0, 0))],
        out_specs=pl.BlockSpec((None, tm, d), lambda a, i: (a, i, 0)),
        compiler_params=_params(("parallel", "parallel")),
        name="prenorm",
    )(x, w.reshape(1, d))


QKV_TILE = 512
SC_TILE_GROUPS = 2


def _pipelined(tasks):
    pending = None
    for matmul, epilogue in tasks:
        val = matmul()
        if pending is not None:
            pending[0](pending[1])
        pending = (epilogue, val)
    pending[0](pending[1])


def _split3(x):
    hi = x.astype(BF16)
    r1 = x - hi.astype(F32)
    mid = r1.astype(BF16)
    lo = (r1 - mid.astype(F32)).astype(BF16)
    return hi, mid, lo


def _inproj_kernel(h_ref, hp_ref, hn_ref, wq_ref, wz_ref, ws_ref, cwq_ref, cws_ref, gain_ref, alog_ref, dtb_ref,
                   qkv_ref, z_ref, g_ref, ysc_ref, *, tm):
    i = pl.program_id(1)
    hx = _with_halo(h_ref, hp_ref, hn_ref, i == 0, i == pl.num_programs(1) - 1)
    h = h_ref[...]
    gd = SC_GROUP_DIM
    sc_tile = SC_TILE_GROUPS * gd

    def halo_matmul(w_ref, c0, width):
        return _bdot(hx, w_ref[:, c0:c0 + width])

    def plain_matmul(w_ref, c0, width):
        return _bdot(h, w_ref[:, c0:c0 + width])

    def gate_epilogue(gl):
        g = -jnp.exp(alog_ref[...]) * _softplus(gl + dtb_ref[...])
        beta = _sigmoid(gl)
        hi, mid, lo = _split3(g)
        r = lax.broadcasted_iota(jnp.int32, (CHUNK, 3 * CHUNK), 0)
        c = lax.broadcasted_iota(jnp.int32, (CHUNK, 3 * CHUNK), 1) % CHUNK
        m_fwd = (c <= r).astype(BF16)
        m_bwd = (c >= r).astype(BF16)
        lane = lax.broadcasted_iota(jnp.int32, (CHUNK, LANES), 1)
        for ci in range(tm // CHUNK):
            rows = slice(ci * CHUNK, (ci + 1) * CHUNK)
            g3 = jnp.concatenate([hi[rows], mid[rows], lo[rows]], axis=0)
            cum_f = _bdot(m_fwd, g3)
            cum_b = _bdot(m_bwd, g3)
            g_ref[rows, :] = jnp.where(lane < DN_HEADS, cum_f,
                                       jnp.where(lane < 2 * DN_HEADS, cum_b, beta[rows]))

    def qkv_epilogue(c0, p):
        y = _silu(_conv3_rows(p, cwq_ref[:, c0:c0 + QKV_TILE], tm))
        if c0 >= 2 * DN_DIM:
            qkv_ref[:, c0:c0 + QKV_TILE] = y.astype(qkv_ref.dtype)
            return
        for hh in range(QKV_TILE // DN_HEAD_DIM):
            yh = y[:, hh * DN_HEAD_DIM:(hh + 1) * DN_HEAD_DIM]
            ss = jnp.sum(yh * yh, axis=-1, keepdims=True)
            sl = slice(c0 + hh * DN_HEAD_DIM, c0 + (hh + 1) * DN_HEAD_DIM)
            qkv_ref[:, sl] = (yh * lax.rsqrt(ss + L2_EPS)).astype(qkv_ref.dtype)

    def z_epilogue(c0, p):
        z_ref[:, c0:c0 + QKV_TILE] = _silu(p).astype(z_ref.dtype)

    def sc_matmul(c0):
        return tuple(_bdot(hx, ws_ref[:, seg * SC_DIM + c0:seg * SC_DIM + c0 + sc_tile]) for seg in range(3))

    def sc_epilogue(c0, bcx):
        b, c, x = bcx
        y = b[HALO:HALO + tm] * _conv3_rows(c * x, cws_ref[:, c0:c0 + sc_tile], tm)
        for gi in range(SC_TILE_GROUPS):
            lo = gi * gd
            g0 = c0 + lo
            ysc_ref[:, g0:g0 + gd] = (_rms(y[:, lo:lo + gd]) * gain_ref[:, g0:g0 + gd]).astype(ysc_ref.dtype)

    part = functools.partial
    tasks = [(part(plain_matmul, wz_ref, DN_DIM, LANES), gate_epilogue)]
    tasks += [(part(halo_matmul, wq_ref, c0, QKV_TILE), part(qkv_epilogue, c0))
              for c0 in range(0, 3 * DN_DIM, QKV_TILE)]
    tasks += [(part(plain_matmul, wz_ref, c0, QKV_TILE), part(z_epilogue, c0))
              for c0 in range(0, DN_DIM, QKV_TILE)]
    tasks += [(part(sc_matmul, c0), part(sc_epilogue, c0)) for c0 in range(0, SC_DIM, sc_tile)]
    _pipelined(tasks)


def _in_proj(h, w_qkv, w_zg, w_sc, cw_qkv, cw_sc, gain, alog, dtb, tm):
    n, t, d = h.shape
    main, prev, nxt = _halo_specs(tm, t, d)
    whole = lambda shape: pl.BlockSpec(shape, lambda a, i: (0, 0))
    row = lambda width: pl.BlockSpec((None, tm, width), lambda a, i: (a, i, 0))
    return pl.pallas_call(
        functools.partial(_inproj_kernel, tm=tm),
        out_shape=(jax.ShapeDtypeStruct((n, t, 3 * DN_DIM), BF16),
                   jax.ShapeDtypeStruct((n, t, DN_DIM), BF16),
                   jax.ShapeDtypeStruct((n, t, LANES), F32),
                   jax.ShapeDtypeStruct((n, t, SC_DIM), BF16)),
        grid=(n, t // tm),
        in_specs=[main, prev, nxt, whole(w_qkv.shape), whole(w_zg.shape), whole(w_sc.shape),
                  whole((3, 3 * DN_DIM)), whole((3, SC_DIM)), whole((1, SC_DIM)),
                  whole((1, LANES)), whole((1, LANES))],
        out_specs=(row(3 * DN_DIM), row(DN_DIM), row(LANES), row(SC_DIM)),
        compiler_params=_params(("parallel", "parallel")),
        name="in_proj",
    )(h, h, h, w_qkv, w_zg, w_sc, cw_qkv, cw_sc, gain.reshape(1, SC_DIM), alog, dtb)


def _delta_kernel(qf_ref, kf_ref, vf_ref, gcf_ref, grf_ref, qb_ref, kb_ref, vb_ref, gcb_ref, grb_ref,
                  of_ref, ob_ref, s_ref, *, tb):
    @pl.when(pl.program_id(1) == 0)
    def _():
        s_ref[...] = jnp.zeros_like(s_ref)

    nch = tb // CHUNK
    unroll = 2 if nch % 2 == 0 else 1
    r = lax.broadcasted_iota(jnp.int32, (CHUNK, CHUNK), 0)
    c = lax.broadcasted_iota(jnp.int32, (CHUNK, CHUNK), 1)
    eye = (r == c).astype(F32)
    masks = ((r >= c, r > c), (r <= c, r < c))
    refs = ((qf_ref, kf_ref, vf_ref, gcf_ref, grf_ref, of_ref),
            (qb_ref, kb_ref, vb_ref, gcb_ref, grb_ref, ob_ref))

    def ld(ch, name):
        return ch[name + "_ref"][ch["rows"], ch["sl"]]

    def chains_of(jj):
        chains = []
        for d, (q_ref, k_ref, v_ref, gc_ref, gr_ref, o_ref) in enumerate(refs):
            cc = (nch - 1 - jj) if d else jj
            rows = pl.ds(pl.multiple_of(cc * CHUNK, CHUNK), CHUNK)
            gcol = gc_ref[rows, :]
            grow = gr_ref[cc]
            goff = d * DN_HEADS
            boff = 2 * DN_HEADS + goff
            cum_c = gcol[:, goff:goff + DN_HEADS]
            b_c = gcol[:, boff:boff + DN_HEADS]
            cum_r = grow[goff:goff + DN_HEADS, :]
            b_r = grow[boff:boff + DN_HEADS, :]
            last = 0 if d else CHUNK - 1
            gl_c = cum_c[last:last + 1, :]
            eg_c = jnp.exp(cum_c) * Q_SCALE
            ekd_c = jnp.exp(gl_c - cum_c)
            egl = jnp.exp(gl_c)
            ebr = b_r * jnp.exp(cum_r)
            for hh in range(DN_HEADS):
                sl = slice(hh * DN_HEAD_DIM, (hh + 1) * DN_HEAD_DIM)
                chains.append(dict(
                    d=d, hh=hh, rows=rows, sl=sl, q_ref=q_ref, k_ref=k_ref, v_ref=v_ref, o_ref=o_ref,
                    mle=masks[d][0], strict=masks[d][1],
                    gc=cum_c[:, hh:hh + 1], gr=cum_r[hh:hh + 1, :], bc=b_c[:, hh:hh + 1], br=b_r[hh:hh + 1, :],
                    egc=eg_c[:, hh:hh + 1], ekd=ekd_c[:, hh:hh + 1], egl=egl[:, hh:hh + 1], ebr=ebr[hh:hh + 1, :]))
        return chains

    def state_free(chains):
        decay = [jnp.where(ch["mle"], jnp.exp(jnp.where(ch["mle"], ch["gc"] - ch["gr"], 0.0)), 0.0)
                 for ch in chains]
        kq = [lax.dot_general(jnp.concatenate([ld(ch, "k"), ld(ch, "q")], axis=0), ld(ch, "k"), NT_DIMS,
                              preferred_element_type=F32) for ch in chains]
        a_qk = [(x[CHUNK:] * dc * Q_SCALE).astype(BF16) for x, dc in zip(kq, decay)]
        pw = [jnp.where(ch["strict"], -(x[:CHUNK] * dc) * ch["bc"], 0.0) for ch, x, dc in zip(chains, kq, decay)]
        tinv = [eye + p for p in pw]
        pw = [_bdot(p.astype(BF16), p.astype(BF16)) for p in pw]
        for _ in range(4):
            both = [_bdot(jnp.concatenate([t, p], axis=0).astype(BF16), p.astype(BF16)) for t, p in zip(tinv, pw)]
            tinv = [t + b[:CHUNK] for t, b in zip(tinv, both)]
            pw = [b[CHUNK:] for b in both]
        tinv = [t + _bdot(t.astype(BF16), p.astype(BF16)) for t, p in zip(tinv, pw)]
        u = [_bdot((t * ch["br"]).astype(BF16), ld(ch, "v")) for ch, t in zip(chains, tinv)]
        w = [_bdot((t * ch["ebr"]).astype(BF16), ld(ch, "k")) for ch, t in zip(chains, tinv)]
        qg = [(ld(ch, "q").astype(F32) * ch["egc"]).astype(BF16) for ch in chains]
        kdec = [(ld(ch, "k").astype(F32) * ch["ekd"]).astype(BF16) for ch in chains]
        return list(zip(a_qk, u, w, qg, kdec))

    def state_step(chains, free, s):
        wq = [_bdot(jnp.concatenate([w.astype(BF16), qg], axis=0), z.astype(BF16))
              for (_, _, w, qg, _), z in zip(free, s)]
        vn = [(u - y[:CHUNK]).astype(BF16) for (_, u, _, _, _), y in zip(free, wq)]
        for ch, y, (a_qk, _, _, _, _), x in zip(chains, wq, free, vn):
            ch["o_ref"][ch["rows"], ch["sl"]] = y[CHUNK:] + _bdot(a_qk, x)
        return [z * ch["egl"] + lax.dot_general(kdec, x, TN_DIMS, preferred_element_type=F32)
                for ch, z, (_, _, _, _, kdec), x in zip(chains, s, free, vn)]

    def body(it, carry):
        chunks = [chains_of(it * unroll + j) for j in range(unroll)]
        free = state_free([ch for chains in chunks for ch in chains])
        s = [s_ref[ch["d"], ch["hh"]] for ch in chunks[0]]
        for j, chains in enumerate(chunks):
            s = state_step(chains, free[j * len(chains):(j + 1) * len(chains)], s)
        for ch, z in zip(chunks[0], s):
            s_ref[ch["d"], ch["hh"]] = z
        return carry

    lax.fori_loop(0, nch // unroll, body, 0)


def _delta_scan(qkv, gates, gates_t, tb):
    n, t, _ = qkv.shape
    nb = t // tb
    nch = tb // CHUNK
    fwd = lambda i: i
    bwd = lambda i: nb - 1 - i

    def specs(blk):
        return [pl.BlockSpec((None, tb, DN_DIM), lambda a, i: (a, blk(i), 0)),
                pl.BlockSpec((None, tb, DN_DIM), lambda a, i: (a, blk(i), 1)),
                pl.BlockSpec((None, tb, DN_DIM), lambda a, i: (a, blk(i), 2)),
                pl.BlockSpec((None, tb, LANES), lambda a, i: (a, blk(i), 0)),
                pl.BlockSpec((None, nch, N_GATE_COLS, CHUNK), lambda a, i: (a, blk(i), 0, 0))]

    out = jax.ShapeDtypeStruct((n, t, DN_DIM), F32)
    args = (qkv, qkv, qkv, gates, gates_t)
    return pl.pallas_call(
        functools.partial(_delta_kernel, tb=tb),
        out_shape=(out, out),
        grid=(n, nb),
        in_specs=specs(fwd) + specs(bwd),
        out_specs=(pl.BlockSpec((None, tb, DN_DIM), lambda a, i: (a, fwd(i), 0)),
                   pl.BlockSpec((None, tb, DN_DIM), lambda a, i: (a, bwd(i), 0))),
        scratch_shapes=[pltpu.VMEM((2, DN_HEADS, DN_HEAD_DIM, DN_HEAD_DIM), F32)],
        compiler_params=_params(("parallel", "arbitrary")),
        name="delta_scan",
    )(*args, *args)


OUT_ROWS = 256


def _out_kernel(of_ref, ob_ref, zs_ref, ysc_ref, x_ref, w_ref, dnw_ref, npost_ref, nnext_ref,
                xo_ref, hn_ref, odn_ref, *, tm):
    def matmul(r0):
        rows = slice(r0, r0 + OUT_ROWS)
        for hh in range(DN_HEADS):
            sl = slice(hh * DN_HEAD_DIM, (hh + 1) * DN_HEAD_DIM)
            o = of_ref[rows, sl] + ob_ref[rows, sl]
            y = _rms(o) * dnw_ref[...] * zs_ref[rows, sl].astype(F32)
            odn_ref[rows, sl] = y.astype(odn_ref.dtype)
        return _bdot(odn_ref[rows, :], w_ref[0:DN_DIM, :]) + _bdot(ysc_ref[rows, :], w_ref[DN_DIM:, :])

    def epilogue(r0, m):
        rows = slice(r0, r0 + OUT_ROWS)
        xn = x_ref[rows, :] + _rms(m) * npost_ref[...]
        xo_ref[rows, :] = xn
        hn_ref[rows, :] = (_rms(xn) * nnext_ref[...]).astype(hn_ref.dtype)

    _pipelined([(functools.partial(matmul, r0), functools.partial(epilogue, r0))
                for r0 in range(0, tm, OUT_ROWS)])


def _out_proj(o_f, o_b, zs, ysc, x, w, dn_w, n_post, n_next, tm):
    n, t, d = x.shape
    assert tm % OUT_ROWS == 0
    row = lambda width: pl.BlockSpec((None, tm, width), lambda a, i: (a, i, 0))
    vec = lambda width: pl.BlockSpec((1, width), lambda a, i: (0, 0))
    return pl.pallas_call(
        functools.partial(_out_kernel, tm=tm),
        out_shape=(jax.ShapeDtypeStruct((n, t, d), F32), jax.ShapeDtypeStruct((n, t, d), BF16)),
        grid=(n, t // tm),
        in_specs=[row(DN_DIM), row(DN_DIM), row(DN_DIM), row(SC_DIM), row(d),
                  pl.BlockSpec((d, d), lambda a, i: (0, 0)),
                  vec(DN_HEAD_DIM), vec(d), vec(d)],
        out_specs=(row(d), row(d)),
        scratch_shapes=[pltpu.VMEM((tm, DN_DIM), BF16)],
        compiler_params=_params(("parallel", "parallel")),
        name="out_proj",
    )(o_f, o_b, zs, ysc, x, w, dn_w.reshape(1, -1), n_post.reshape(1, d), n_next.reshape(1, d))


def _ffn_kernel(h_ref, hp_ref, hn_ref, wa_ref, wb_ref, cw_ref, wd_ref, x_ref, npost_ref, nnext_ref,
                xo_ref, *rest, tm, emit_next):
    ho_ref = rest[0] if emit_next else None
    hx_ref, acc_ref = rest[-2:]
    i = pl.program_id(1)
    f = pl.program_id(2)

    @pl.when(f == 0)
    def _():
        _fill_halo(hx_ref, h_ref, hp_ref, hn_ref, tm, i == 0, i == pl.num_programs(1) - 1)
        acc_ref[...] = jnp.zeros_like(acc_ref)

    a = _bdot(hx_ref[...], wa_ref[...])
    b = _bdot(h_ref[...], wb_ref[...])
    act = (_silu(_conv3_rows(a, cw_ref[...], tm)) * b).astype(BF16)
    acc_ref[...] += _bdot(act, wd_ref[...])

    @pl.when(f == pl.num_programs(2) - 1)
    def _():
        xn = x_ref[...] + _rms(acc_ref[...]) * npost_ref[...]
        xo_ref[...] = xn
        if emit_next:
            ho_ref[...] = (_rms(xn) * nnext_ref[...]).astype(ho_ref.dtype)


def _ffn(h, x, w_up, cw, w_down, n_post, n_next, emit_next, tm, tf=512):
    n, t, d = x.shape
    nf = D_FF // tf
    main, prev, nxt = _halo_specs(tm, t, d)
    row = lambda: pl.BlockSpec((None, tm, d), lambda a, i, f: (a, i, 0))
    vec = lambda: pl.BlockSpec((1, d), lambda a, i, f: (0, 0))
    outs = (jax.ShapeDtypeStruct((n, t, d), F32), jax.ShapeDtypeStruct((n, t, d), BF16))[:2 if emit_next else 1]
    return pl.pallas_call(
        functools.partial(_ffn_kernel, tm=tm, emit_next=emit_next),
        out_shape=outs,
        grid=(n, t // tm, nf),
        in_specs=[main, prev, nxt,
                  pl.BlockSpec((d, tf), lambda a, i, f: (0, f)),
                  pl.BlockSpec((d, tf), lambda a, i, f: (0, f + nf)),
                  pl.BlockSpec((3, tf), lambda a, i, f: (0, f)),
                  pl.BlockSpec((tf, d), lambda a, i, f: (f, 0)),
                  row(), vec(), vec()],
        out_specs=tuple(row() for _ in outs),
        scratch_shapes=[pltpu.VMEM((tm + 2 * HALO, d), BF16), pltpu.VMEM((tm, d), F32)],
        compiler_params=_params(("parallel", "parallel", "arbitrary")),
        name="ffn",
    )(h, h, h, w_up, w_up, cw, w_down, x, n_post.reshape(1, d), n_next.reshape(1, d))


def _tiles(t):
    return min(512, t), min(256, t)


def _prepare(w_in, a_log, dt_bias):
    depth = w_in.shape[0]
    w_qkv = w_in[:, :, :3 * DN_DIM].astype(BF16)
    z_end = 4 * DN_DIM
    g_end = z_end + N_GATE_COLS
    pad = jnp.zeros((depth, D_MODEL, LANES - N_GATE_COLS), w_in.dtype)
    w_zg = jnp.concatenate([w_in[:, :, 3 * DN_DIM:g_end], pad], axis=-1).astype(BF16)
    w_sc = w_in[:, :, g_end:].astype(BF16)
    vpad = jnp.zeros((depth, 1, LANES - 2 * DN_HEADS), F32)
    alog = jnp.concatenate([a_log.reshape(depth, 1, 2 * DN_HEADS).astype(F32), vpad], axis=-1)
    dtb = jnp.concatenate([dt_bias.reshape(depth, 1, 2 * DN_HEADS).astype(F32), vpad], axis=-1)
    return w_qkv, w_zg, w_sc, alog, dtb


def _trunk(x, prm):
    (norm_mix_pre, w_qkv, w_zg, w_sc, alog, dtb, conv_qkv, dn_norm, conv_sc, sc_norm, w_out,
     norm_mix_post, norm_ffn_pre, w_up, conv_ffn, w_down, norm_ffn_post) = prm
    depth = w_qkv.shape[0]
    n, t, _ = x.shape
    tm, tb = _tiles(t)
    h = _prenorm(x, norm_mix_pre[0], tm)
    for l in range(depth):
        qkv, zs, gates, ysc = _in_proj(h, w_qkv[l], w_zg[l], w_sc[l], conv_qkv[l], conv_sc[l], sc_norm[l],
                                       alog[l], dtb[l], tm)
        gates_t = gates[:, :, :N_GATE_COLS].reshape(n, t // CHUNK, CHUNK, N_GATE_COLS).transpose(0, 1, 3, 2)
        o_f, o_b = _delta_scan(qkv, gates, gates_t, tb)
        x, h = _out_proj(o_f, o_b, zs, ysc, x, w_out[l], dn_norm[l], norm_mix_post[l], norm_ffn_pre[l], tm)
        last = l == depth - 1
        x, *h = _ffn(h, x, w_up[l], conv_ffn[l], w_down[l], norm_ffn_post[l],
                     norm_mix_pre[0 if last else l + 1], not last, tm)
        h = h[0] if h else None
    return x


def kernel(x_prompt, x_sample, norm_mix_pre, w_in, conv_qkv, a_log, dt_bias, dn_norm, conv_sc, sc_norm, w_out, norm_mix_post, norm_ffn_pre, w_up, conv_ffn, w_down, norm_ffn_post):
    w_qkv, w_zg, w_sc, alog, dtb = _prepare(w_in, a_log, dt_bias)
    prm = (norm_mix_pre, w_qkv, w_zg, w_sc, alog, dtb, conv_qkv, dn_norm, conv_sc, sc_norm,
           w_out.astype(BF16), norm_mix_post, norm_ffn_pre, w_up.astype(BF16), conv_ffn,
           w_down.astype(BF16), norm_ffn_post)
    return (_trunk(x_prompt, prm), _trunk(x_sample, prm))
```

```python
import functools

import jax
import jax.numpy as jnp
from jax import lax
from jax.experimental import pallas as pl
from jax.experimental.pallas import tpu as pltpu

D_MODEL = 2048
DN_HEADS = 8
DN_HEAD_DIM = 128
DN_DIM = DN_HEADS * DN_HEAD_DIM
SC_GROUPS = 8
SC_GROUP_DIM = 128
SC_DIM = SC_GROUPS * SC_GROUP_DIM
D_FF = 5632
CHUNK = 64
N_GATE_COLS = 4 * DN_HEADS
NORM_EPS = 1e-6
L2_EPS = 1e-6
Q_SCALE = DN_HEAD_DIM ** -0.5

LANES = 128
HALO = 16
VMEM_LIMIT = 56 * 1024 * 1024

F32 = jnp.float32
BF16 = jnp.bfloat16
NT_DIMS = (((1,), (1,)), ((), ()))
TN_DIMS = (((0,), (0,)), ((), ()))


def _sigmoid(x):
    return 1.0 / (1.0 + jnp.exp(-x))


def _silu(x):
    return x * _sigmoid(x)


def _softplus(x):
    return jnp.maximum(x, 0.0) + jnp.log1p(jnp.exp(-jnp.abs(x)))


def _rms(x):
    return x * lax.rsqrt(jnp.mean(x * x, axis=-1, keepdims=True) + NORM_EPS)


def _bdot(a, b):
    return jnp.dot(a, b, preferred_element_type=F32)


def _conv3_rows(p, cw, tm):
    rows = p.shape[0]
    prev = pltpu.roll(p, 1, 0)[HALO:HALO + tm]
    nxt = pltpu.roll(p, rows - 1, 0)[HALO:HALO + tm]
    mid = p[HALO:HALO + tm]
    return prev * cw[0:1] + mid * cw[1:2] + nxt * cw[2:3]


def _fill_halo(hx_ref, h_ref, hp_ref, hn_ref, tm, first, last):
    hp = hp_ref[...]
    hn = hn_ref[...]
    hx_ref[0:HALO, :] = jnp.where(first, jnp.zeros_like(hp), hp)
    hx_ref[HALO:HALO + tm, :] = h_ref[...]
    hx_ref[HALO + tm:HALO + tm + HALO, :] = jnp.where(last, jnp.zeros_like(hn), hn)


def _with_halo(h_ref, hp_ref, hn_ref, first, last):
    hp = hp_ref[...]
    hn = hn_ref[...]
    return jnp.concatenate([jnp.where(first, jnp.zeros_like(hp), hp), h_ref[...],
                            jnp.where(last, jnp.zeros_like(hn), hn)], axis=0)


def _halo_specs(tm, t_len, width):
    per = tm // HALO
    nblk = t_len // HALO
    main = pl.BlockSpec((None, tm, width), lambda n, i, *_: (n, i, 0))
    prev = pl.BlockSpec((None, HALO, width), lambda n, i, *_: (n, jnp.maximum(i * per - 1, 0), 0))
    nxt = pl.BlockSpec((None, HALO, width), lambda n, i, *_: (n, jnp.minimum((i + 1) * per, nblk - 1), 0))
    return main, prev, nxt


def _params(sem):
    return pltpu.CompilerParams(dimension_semantics=sem, vmem_limit_bytes=VMEM_LIMIT)


def _prenorm_kernel(x_ref, w_ref, o_ref):
    o_ref[...] = (_rms(x_ref[...]) * w_ref[...]).astype(o_ref.dtype)


def _prenorm(x, w, tm):
    n, t, d = x.shape
    return pl.pallas_call(
        _prenorm_kernel,
        out_shape=jax.ShapeDtypeStruct((n, t, d), BF16),
        grid=(n, t // tm),
        in_specs=[pl.BlockSpec((None, tm, d), lambda a, i: (a, i, 0)),
                  pl.BlockSpec((1, d), lambda a, i: (0, 0))],
        out_specs=pl.BlockSpec((None, tm, d), lambda a, i: (a, i, 0)),
        compiler_params=_params(("parallel", "parallel")),
        name="prenorm",
    )(x, w.reshape(1, d))


QKV_TILE = 512
SC_TILE_GROUPS = 2


def _pipelined(tasks):
    pending = None
    for matmul, epilogue in tasks:
        val = matmul()
        if pending is not None:
            pending[0](pending[1])
        pending = (epilogue, val)
    pending[0](pending[1])


def _split3(x):
    hi = x.astype(BF16)
    r1 = x - hi.astype(F32)
    mid = r1.astype(BF16)
    lo = (r1 - mid.astype(F32)).astype(BF16)
    return hi, mid, lo


def _inproj_kernel(h_ref, hp_ref, hn_ref, wq_ref, wz_ref, ws_ref, cwq_ref, cws_ref, gain_ref, alog_ref, dtb_ref,
                   qkv_ref, z_ref, g_ref, ysc_ref, kt_ref, *, tm):
    i = pl.program_id(1)
    hx = _with_halo(h_ref, hp_ref, hn_ref, i == 0, i == pl.num_programs(1) - 1)
    h = h_ref[...]
    gd = SC_GROUP_DIM
    sc_tile = SC_TILE_GROUPS * 3 * gd

    def halo_matmul(w_ref, c0, width):
        return _bdot(hx, w_ref[:, c0:c0 + width])

    def plain_matmul(w_ref, c0, width):
        return _bdot(h, w_ref[:, c0:c0 + width])

    def gate_epilogue(gl):
        g = -jnp.exp(alog_ref[...]) * _softplus(gl + dtb_ref[...])
        beta = _sigmoid(gl)
        hi, mid, lo = _split3(g)
        r = lax.broadcasted_iota(jnp.int32, (CHUNK, 3 * CHUNK), 0)
        c = lax.broadcasted_iota(jnp.int32, (CHUNK, 3 * CHUNK), 1) % CHUNK
        m_fwd = (c <= r).astype(BF16)
        m_bwd = (c >= r).astype(BF16)
        lane = lax.broadcasted_iota(jnp.int32, (CHUNK, LANES), 1)
        for ci in range(tm // CHUNK):
            rows = slice(ci * CHUNK, (ci + 1) * CHUNK)
            g3 = jnp.concatenate([hi[rows], mid[rows], lo[rows]], axis=0)
            cum_f = _bdot(m_fwd, g3)
            cum_b = _bdot(m_bwd, g3)
            g_ref[rows, :] = jnp.where(lane < DN_HEADS, cum_f,
                                       jnp.where(lane < 2 * DN_HEADS, cum_b, beta[rows]))

    def qkv_epilogue(c0, p):
        y = _silu(_conv3_rows(p, cwq_ref[:, c0:c0 + QKV_TILE], tm))
        if c0 >= 2 * DN_DIM:
            qkv_ref[:, c0:c0 + QKV_TILE] = y.astype(qkv_ref.dtype)
            return
        heads = []
        for hh in range(QKV_TILE // DN_HEAD_DIM):
            yh = y[:, hh * DN_HEAD_DIM:(hh + 1) * DN_HEAD_DIM]
            ss = jnp.sum(yh * yh, axis=-1, keepdims=True)
            sl = slice(c0 + hh * DN_HEAD_DIM, c0 + (hh + 1) * DN_HEAD_DIM)
            heads.append(yh * lax.rsqrt(ss + L2_EPS))
            qkv_ref[:, sl] = heads[-1].astype(qkv_ref.dtype)
        if c0 >= DN_DIM:
            pair0 = (c0 - DN_DIM) // (2 * DN_HEAD_DIM)
            for ci in range(tm // CHUNK):
                rows = slice(ci * CHUNK, (ci + 1) * CHUNK)
                for pp in range(len(heads) // 2):
                    both = jnp.concatenate([heads[2 * pp][rows], heads[2 * pp + 1][rows]], axis=0)
                    kt_ref[ci, pair0 + pp] = both.T.astype(kt_ref.dtype)

    def z_epilogue(c0, p):
        z_ref[:, c0:c0 + QKV_TILE] = _silu(p).astype(z_ref.dtype)

    def sc_epilogue(jn, p):
        for gi in range(SC_TILE_GROUPS):
            base = gi * 3 * gd
            b = p[HALO:HALO + tm, base:base + gd]
            cx = p[:, base + gd:base + 2 * gd] * p[:, base + 2 * gd:base + 3 * gd]
            g0 = (jn * SC_TILE_GROUPS + gi) * gd
            y = b * _conv3_rows(cx, cws_ref[:, g0:g0 + gd], tm)
            ysc_ref[:, g0:g0 + gd] = (_rms(y) * gain_ref[:, g0:g0 + gd]).astype(ysc_ref.dtype)

    part = functools.partial
    tasks = [(part(plain_matmul, wz_ref, DN_DIM, LANES), gate_epilogue)]
    tasks += [(part(halo_matmul, wq_ref, c0, QKV_TILE), part(qkv_epilogue, c0))
              for c0 in range(0, 3 * DN_DIM, QKV_TILE)]
    tasks += [(part(plain_matmul, wz_ref, c0, QKV_TILE), part(z_epilogue, c0))
              for c0 in range(0, DN_DIM, QKV_TILE)]
    tasks += [(part(halo_matmul, ws_ref, jn * sc_tile, sc_tile), part(sc_epilogue, jn))
              for jn in range(SC_GROUPS // SC_TILE_GROUPS)]
    _pipelined(tasks)


def _in_proj(h, w_qkv, w_zg, w_sc, cw_qkv, cw_sc, gain, alog, dtb, tm):
    n, t, d = h.shape
    main, prev, nxt = _halo_specs(tm, t, d)
    whole = lambda shape: pl.BlockSpec(shape, lambda a, i: (0, 0))
    row = lambda width: pl.BlockSpec((None, tm, width), lambda a, i: (a, i, 0))
    return pl.pallas_call(
        functools.partial(_inproj_kernel, tm=tm),
        out_shape=(jax.ShapeDtypeStruct((n, t, 3 * DN_DIM), BF16),
                   jax.ShapeDtypeStruct((n, t, DN_DIM), BF16),
                   jax.ShapeDtypeStruct((n, t, LANES), F32),
                   jax.ShapeDtypeStruct((n, t, SC_DIM), BF16),
                   jax.ShapeDtypeStruct((n, t // CHUNK, DN_HEADS // 2, DN_HEAD_DIM, 2 * CHUNK), BF16)),
        grid=(n, t // tm),
        in_specs=[main, prev, nxt, whole(w_qkv.shape), whole(w_zg.shape), whole(w_sc.shape),
                  whole((3, 3 * DN_DIM)), whole((3, SC_DIM)), whole((1, SC_DIM)),
                  whole((1, LANES)), whole((1, LANES))],
        out_specs=(row(3 * DN_DIM), row(DN_DIM), row(LANES), row(SC_DIM),
                   pl.BlockSpec((None, tm // CHUNK, DN_HEADS // 2, DN_HEAD_DIM, 2 * CHUNK),
                                lambda a, i: (a, i, 0, 0, 0))),
        compiler_params=_params(("parallel", "parallel")),
        name="in_proj",
    )(h, h, h, w_qkv, w_zg, w_sc, cw_qkv, cw_sc, gain.reshape(1, SC_DIM), alog, dtb)


def _delta_kernel(qf_ref, kf_ref, vf_ref, gcf_ref, grf_ref, ktf_ref, qb_ref, kb_ref, vb_ref, gcb_ref, grb_ref,
                  ktb_ref, of_ref, ob_ref, s_ref, *, tb):
    @pl.when(pl.program_id(1) == 0)
    def _():
        s_ref[...] = jnp.zeros_like(s_ref)

    nch = tb // CHUNK
    unroll = 4 if nch % 4 == 0 else 1
    pair_w = 2 * DN_HEAD_DIM
    r = lax.broadcasted_iota(jnp.int32, (CHUNK, 2 * CHUNK), 0)
    lane = lax.broadcasted_iota(jnp.int32, (CHUNK, 2 * CHUNK), 1)
    c = lane % CHUNK
    second = lane >= CHUNK
    eye = (r == c).astype(F32)
    masks = ((r >= c, r > c), (r <= c, r < c))
    row_head = lambda width: lax.broadcasted_iota(jnp.int32, (2 * CHUNK, width), 0) // CHUNK
    col_head = lambda width, per: lax.broadcasted_iota(jnp.int32, (2 * CHUNK, width), 1) // per
    diag_sq = (row_head(2 * CHUNK) == col_head(2 * CHUNK, CHUNK)).astype(BF16)
    diag_wide = (row_head(pair_w) == col_head(pair_w, DN_HEAD_DIM)).astype(BF16)
    zero_s = jnp.zeros((DN_HEAD_DIM, DN_HEAD_DIM), BF16)
    refs = ((qf_ref, kf_ref, vf_ref, gcf_ref, grf_ref, ktf_ref, of_ref),
            (qb_ref, kb_ref, vb_ref, gcb_ref, grb_ref, ktb_ref, ob_ref))

    def ld(ch, name):
        return ch[name + "_ref"][ch["rows"], ch["sl"]]

    def block_diag(x, sel):
        return jnp.concatenate([x, x], axis=0) * sel

    def pair_cols(x, h0):
        shape = (CHUNK, 2 * CHUNK)
        return jnp.where(second, jnp.broadcast_to(x[:, h0 + 1:h0 + 2], shape), jnp.broadcast_to(x[:, h0:h0 + 1], shape))

    def chains_of(jj):
        chains = []
        for d, (q_ref, k_ref, v_ref, gc_ref, gr_ref, kt_ref, o_ref) in enumerate(refs):
            cc = (nch - 1 - jj) if d else jj
            rows = pl.ds(pl.multiple_of(cc * CHUNK, CHUNK), CHUNK)
            gcol = gc_ref[rows, :]
            grow = gr_ref[cc]
            goff = d * DN_HEADS
            boff = 2 * DN_HEADS + goff
            cum_c = gcol[:, goff:goff + DN_HEADS]
            b_c = gcol[:, boff:boff + DN_HEADS]
            last = 0 if d else CHUNK - 1
            gl_c = cum_c[last:last + 1, :]
            eg_c = jnp.exp(cum_c) * Q_SCALE
            ekd_c = jnp.exp(gl_c - cum_c)
            egl = jnp.exp(gl_c)
            npair = DN_HEADS // 2
            for p in range(npair):
                h0 = 2 * p
                gr2 = grow[d * npair + p:d * npair + p + 1, :]
                br2 = grow[(2 + d) * npair + p:(2 + d) * npair + p + 1, :]
                chains.append(dict(
                    d=d, h0=h0, rows=rows, sl=slice(p * pair_w, (p + 1) * pair_w),
                    q_ref=q_ref, k_ref=k_ref, v_ref=v_ref, o_ref=o_ref, mle=masks[d][0], strict=masks[d][1],
                    kt_ref=kt_ref, cc=cc, p=p,
                    gc=pair_cols(cum_c, h0), bc=pair_cols(b_c, h0), gr=gr2, br=br2, ebr=br2 * jnp.exp(gr2),
                    egc=[eg_c[:, h0 + j:h0 + j + 1] for j in range(2)],
                    ekd=[ekd_c[:, h0 + j:h0 + j + 1] for j in range(2)],
                    egl=[egl[:, h0 + j:h0 + j + 1] for j in range(2)]))
        return chains

    def head(x, j):
        return x[:, j * DN_HEAD_DIM:(j + 1) * DN_HEAD_DIM]

    def bd_sq(x):
        return block_diag(x.astype(BF16), diag_sq)

    def state_free(chains):
        decay = [jnp.where(ch["mle"], jnp.exp(jnp.where(ch["mle"], ch["gc"] - ch["gr"], 0.0)), 0.0)
                 for ch in chains]
        bdk = [block_diag(ld(ch, "k"), diag_wide) for ch in chains]
        kq = [lax.dot_general(jnp.concatenate([ld(ch, "k"), ld(ch, "q")], axis=0), x, NT_DIMS,
                              preferred_element_type=F32) for ch, x in zip(chains, bdk)]
        a_qk = [(x[CHUNK:] * dc * Q_SCALE).astype(BF16) for x, dc in zip(kq, decay)]
        pw = [jnp.where(ch["strict"], -(x[:CHUNK] * dc) * ch["bc"], 0.0) for ch, x, dc in zip(chains, kq, decay)]
        tinv = [eye + p for p in pw]
        pw = [_bdot(p.astype(BF16), bd_sq(p)) for p in pw]
        for _ in range(4):
            both = [_bdot(jnp.concatenate([t, p], axis=0).astype(BF16), bd_sq(p)) for t, p in zip(tinv, pw)]
            tinv = [t + b[:CHUNK] for t, b in zip(tinv, both)]
            pw = [b[CHUNK:] for b in both]
        tinv = [t + _bdot(t.astype(BF16), bd_sq(p)) for t, p in zip(tinv, pw)]
        bdv = [block_diag(ld(ch, "v"), diag_wide) for ch in chains]
        u = [_bdot((t * ch["br"]).astype(BF16), x) for ch, t, x in zip(chains, tinv, bdv)]
        w = [_bdot((t * ch["ebr"]).astype(BF16), x) for ch, t, x in zip(chains, tinv, bdk)]
        qg = [jnp.concatenate([(head(ld(ch, "q"), j).astype(F32) * ch["egc"][j]).astype(BF16) for j in range(2)],
                              axis=1) for ch in chains]
        return list(zip(a_qk, u, w, qg))

    def state_step(chains, free, s):
        s_bd = [jnp.concatenate([jnp.concatenate([z[0].astype(BF16), zero_s], axis=1),
                                 jnp.concatenate([zero_s, z[1].astype(BF16)], axis=1)], axis=0) for z in s]
        wq = [_bdot(jnp.concatenate([w.astype(BF16), qg], axis=0), z)
              for (_, _, w, qg), z in zip(free, s_bd)]
        vn = [u - y[:CHUNK] for (_, u, _, _), y in zip(free, wq)]
        for ch, y, (a_qk, _, _, _), x in zip(chains, wq, free, vn):
            ch["o_ref"][ch["rows"], ch["sl"]] = y[CHUNK:] + _bdot(a_qk, block_diag(x.astype(BF16), diag_wide))
        vd = [jnp.concatenate([(head(x, j) * ch["ekd"][j]).astype(BF16) for j in range(2)], axis=1)
              for ch, x in zip(chains, vn)]
        upd = [_bdot(ch["kt_ref"][ch["cc"], ch["p"]], block_diag(x, diag_wide)) for ch, x in zip(chains, vd)]
        return [[z[j] * ch["egl"][j] + head(y, j) for j in range(2)] for ch, z, y in zip(chains, s, upd)]

    def body(it, carry):
        chunks = [chains_of(it * unroll + j) for j in range(unroll)]
        free = state_free([ch for chains in chunks for ch in chains])
        s = [[s_ref[ch["d"], ch["h0"] + j] for j in range(2)] for ch in chunks[0]]
        for j, chains in enumerate(chunks):
            s = state_step(chains, free[j * len(chains):(j + 1) * len(chains)], s)
        for ch, z in zip(chunks[0], s):
            for j in range(2):
                s_ref[ch["d"], ch["h0"] + j] = z[j]
        return carry

    lax.fori_loop(0, nch // unroll, body, 0)


def _delta_scan(qkv, gates, gates_t, k_t, tb):
    n, t, _ = qkv.shape
    nb = t // tb
    nch = tb // CHUNK
    fwd = lambda i: i
    bwd = lambda i: nb - 1 - i

    def specs(blk):
        return [pl.BlockSpec((None, tb, DN_DIM), lambda a, i: (a, blk(i), 0)),
                pl.BlockSpec((None, tb, DN_DIM), lambda a, i: (a, blk(i), 1)),
                pl.BlockSpec((None, tb, DN_DIM), lambda a, i: (a, blk(i), 2)),
                pl.BlockSpec((None, tb, LANES), lambda a, i: (a, blk(i), 0)),
                pl.BlockSpec((None, nch, N_GATE_COLS // 2, 2 * CHUNK), lambda a, i: (a, blk(i), 0, 0)),
                pl.BlockSpec((None, nch, DN_HEADS // 2, DN_HEAD_DIM, 2 * CHUNK), lambda a, i: (a, blk(i), 0, 0, 0))]

    out = jax.ShapeDtypeStruct((n, t, DN_DIM), F32)
    args = (qkv, qkv, qkv, gates, gates_t, k_t)
    return pl.pallas_call(
        functools.partial(_delta_kernel, tb=tb),
        out_shape=(out, out),
        grid=(n, nb),
        in_specs=specs(fwd) + specs(bwd),
        out_specs=(pl.BlockSpec((None, tb, DN_DIM), lambda a, i: (a, fwd(i), 0)),
                   pl.BlockSpec((None, tb, DN_DIM), lambda a, i: (a, bwd(i), 0))),
        scratch_shapes=[pltpu.VMEM((2, DN_HEADS, DN_HEAD_DIM, DN_HEAD_DIM), F32)],
        compiler_params=_params(("parallel", "arbitrary")),
        name="delta_scan",
    )(*args, *args)


OUT_ROWS = 256


def _out_kernel(of_ref, ob_ref, zs_ref, ysc_ref, x_ref, w_ref, dnw_ref, npost_ref, nnext_ref,
                xo_ref, hn_ref, odn_ref, *, tm):
    def matmul(r0):
        rows = slice(r0, r0 + OUT_ROWS)
        for hh in range(DN_HEADS):
            sl = slice(hh * DN_HEAD_DIM, (hh + 1) * DN_HEAD_DIM)
            o = of_ref[rows, sl] + ob_ref[rows, sl]
            y = _rms(o) * dnw_ref[...] * zs_ref[rows, sl].astype(F32)
            odn_ref[rows, sl] = y.astype(odn_ref.dtype)
        return _bdot(odn_ref[rows, :], w_ref[0:DN_DIM, :]) + _bdot(ysc_ref[rows, :], w_ref[DN_DIM:, :])

    def epilogue(r0, m):
        rows = slice(r0, r0 + OUT_ROWS)
        xn = x_ref[rows, :] + _rms(m) * npost_ref[...]
        xo_ref[rows, :] = xn
        hn_ref[rows, :] = (_rms(xn) * nnext_ref[...]).astype(hn_ref.dtype)

    _pipelined([(functools.partial(matmul, r0), functools.partial(epilogue, r0))
                for r0 in range(0, tm, OUT_ROWS)])


def _out_proj(o_f, o_b, zs, ysc, x, w, dn_w, n_post, n_next, tm):
    n, t, d = x.shape
    assert tm % OUT_ROWS == 0
    row = lambda width: pl.BlockSpec((None, tm, width), lambda a, i: (a, i, 0))
    vec = lambda width: pl.BlockSpec((1, width), lambda a, i: (0, 0))
    return pl.pallas_call(
        functools.partial(_out_kernel, tm=tm),
        out_shape=(jax.ShapeDtypeStruct((n, t, d), F32), jax.ShapeDtypeStruct((n, t, d), BF16)),
        grid=(n, t // tm),
        in_specs=[row(DN_DIM), row(DN_DIM), row(DN_DIM), row(SC_DIM), row(d),
                  pl.BlockSpec((d, d), lambda a, i: (0, 0)),
                  vec(DN_HEAD_DIM), vec(d), vec(d)],
        out_specs=(row(d), row(d)),
        scratch_shapes=[pltpu.VMEM((tm, DN_DIM), BF16)],
        compiler_params=_params(("parallel", "parallel")),
        name="out_proj",
    )(o_f, o_b, zs, ysc, x, w, dn_w.reshape(1, -1), n_post.reshape(1, d), n_next.reshape(1, d))


def _ffn_kernel(h_ref, hp_ref, hn_ref, wa_ref, wb_ref, cw_ref, wd_ref, x_ref, npost_ref, nnext_ref,
                xo_ref, *rest, tm, emit_next):
    ho_ref = rest[0] if emit_next else None
    hx_ref, acc_ref = rest[-2:]
    i = pl.program_id(1)
    f = pl.program_id(2)

    @pl.when(f == 0)
    def _():
        _fill_halo(hx_ref, h_ref, hp_ref, hn_ref, tm, i == 0, i == pl.num_programs(1) - 1)
        acc_ref[...] = jnp.zeros_like(acc_ref)

    a = _bdot(hx_ref[...], wa_ref[...])
    b = _bdot(h_ref[...], wb_ref[...])
    act = (_silu(_conv3_rows(a, cw_ref[...], tm)) * b).astype(BF16)
    acc_ref[...] += _bdot(act, wd_ref[...])

    @pl.when(f == pl.num_programs(2) - 1)
    def _():
        xn = x_ref[...] + _rms(acc_ref[...]) * npost_ref[...]
        xo_ref[...] = xn
        if emit_next:
            ho_ref[...] = (_rms(xn) * nnext_ref[...]).astype(ho_ref.dtype)


def _ffn(h, x, w_up, cw, w_down, n_post, n_next, emit_next, tm, tf=512):
    n, t, d = x.shape
    nf = D_FF // tf
    main, prev, nxt = _halo_specs(tm, t, d)
    row = lambda: pl.BlockSpec((None, tm, d), lambda a, i, f: (a, i, 0))
    vec = lambda: pl.BlockSpec((1, d), lambda a, i, f: (0, 0))
    outs = (jax.ShapeDtypeStruct((n, t, d), F32), jax.ShapeDtypeStruct((n, t, d), BF16))[:2 if emit_next else 1]
    return pl.pallas_call(
        functools.partial(_ffn_kernel, tm=tm, emit_next=emit_next),
        out_shape=outs,
        grid=(n, t // tm, nf),
        in_specs=[main, prev, nxt,
                  pl.BlockSpec((d, tf), lambda a, i, f: (0, f)),
                  pl.BlockSpec((d, tf), lambda a, i, f: (0, f + nf)),
                  pl.BlockSpec((3, tf), lambda a, i, f: (0, f)),
                  pl.BlockSpec((tf, d), lambda a, i, f: (f, 0)),
                  row(), vec(), vec()],
        out_specs=tuple(row() for _ in outs),
        scratch_shapes=[pltpu.VMEM((tm + 2 * HALO, d), BF16), pltpu.VMEM((tm, d), F32)],
        compiler_params=_params(("parallel", "parallel", "arbitrary")),
        name="ffn",
    )(h, h, h, w_up, w_up, cw, w_down, x, n_post.reshape(1, d), n_next.reshape(1, d))


def _tiles(t):
    return min(512, t), min(256, t)


def _prepare(w_in, a_log, dt_bias):
    depth = w_in.shape[0]
    w_qkv = w_in[:, :, :3 * DN_DIM].astype(BF16)
    z_end = 4 * DN_DIM
    g_end = z_end + N_GATE_COLS
    pad = jnp.zeros((depth, D_MODEL, LANES - N_GATE_COLS), w_in.dtype)
    w_zg = jnp.concatenate([w_in[:, :, 3 * DN_DIM:g_end], pad], axis=-1).astype(BF16)
    w_sc = w_in[:, :, g_end:].reshape(depth, D_MODEL, 3, SC_GROUPS, SC_GROUP_DIM)
    w_sc = w_sc.transpose(0, 1, 3, 2, 4).reshape(depth, D_MODEL, 3 * SC_DIM).astype(BF16)
    vpad = jnp.zeros((depth, 1, LANES - 2 * DN_HEADS), F32)
    alog = jnp.concatenate([a_log.reshape(depth, 1, 2 * DN_HEADS).astype(F32), vpad], axis=-1)
    dtb = jnp.concatenate([dt_bias.reshape(depth, 1, 2 * DN_HEADS).astype(F32), vpad], axis=-1)
    return w_qkv, w_zg, w_sc, alog, dtb


def _trunk(x, prm):
    (norm_mix_pre, w_qkv, w_zg, w_sc, alog, dtb, conv_qkv, dn_norm, conv_sc, sc_norm, w_out,
     norm_mix_post, norm_ffn_pre, w_up, conv_ffn, w_down, norm_ffn_post) = prm
    depth = w_qkv.shape[0]
    n, t, _ = x.shape
    tm, tb = _tiles(t)
    h = _prenorm(x, norm_mix_pre[0], tm)
    for l in range(depth):
        qkv, zs, gates, ysc, k_t = _in_proj(h, w_qkv[l], w_zg[l], w_sc[l], conv_qkv[l], conv_sc[l], sc_norm[l],
                                            alog[l], dtb[l], tm)
        gates_t = gates[:, :, :N_GATE_COLS].reshape(n, t // CHUNK, CHUNK, 4, DN_HEADS // 2, 2)
        gates_t = gates_t.transpose(0, 1, 3, 4, 5, 2).reshape(n, t // CHUNK, N_GATE_COLS // 2, 2 * CHUNK)
        o_f, o_b = _delta_scan(qkv, gates, gates_t, k_t, tb)
        x, h = _out_proj(o_f, o_b, zs, ysc, x, w_out[l], dn_norm[l], norm_mix_post[l], norm_ffn_pre[l], tm)
        last = l == depth - 1
        x, *h = _ffn(h, x, w_up[l], conv_ffn[l], w_down[l], norm_ffn_post[l],
                     norm_mix_pre[0 if last else l + 1], not last, tm)
        h = h[0] if h else None
    return x


def kernel(x_prompt, x_sample, norm_mix_pre, w_in, conv_qkv, a_log, dt_bias, dn_norm, conv_sc, sc_norm, w_out, norm_mix_post, norm_ffn_pre, w_up, conv_ffn, w_down, norm_ffn_post):
    w_qkv, w_zg, w_sc, alog, dtb = _prepare(w_in, a_log, dt_bias)
    prm = (norm_mix_pre, w_qkv, w_zg, w_sc, alog, dtb, conv_qkv, dn_norm, conv_sc, sc_norm,
           w_out.astype(BF16), norm_mix_post, norm_ffn_pre, w_up.astype(BF16), conv_ffn,
           w_down.astype(BF16), norm_ffn_post)
    return (_trunk(x_prompt, prm), _trunk(x_sample, prm))
```

```python
import functools
import math

import jax
import jax.numpy as jnp
from jax import lax
from jax.experimental import pallas as pl
from jax.experimental.pallas import tpu as pltpu

D_MODEL = 2048
DN_HEADS = 8
DN_HEAD_DIM = 128
DN_DIM = DN_HEADS * DN_HEAD_DIM
SC_GROUPS = 8
SC_GROUP_DIM = 128
SC_DIM = SC_GROUPS * SC_GROUP_DIM
D_FF = 5632
CHUNK = 64
N_GATE_COLS = 4 * DN_HEADS
NORM_EPS = 1e-6
L2_EPS = 1e-6
Q_SCALE = DN_HEAD_DIM ** -0.5

LANES = 128
HALO = 16
VMEM_LIMIT = 56 * 1024 * 1024

F32 = jnp.float32
BF16 = jnp.bfloat16
NT_DIMS = (((1,), (1,)), ((), ()))
TN_DIMS = (((0,), (0,)), ((), ()))


def _sigmoid(x):
    return 1.0 / (1.0 + jnp.exp(-x))


def _silu(x):
    return x * _sigmoid(x)


def _softplus(x):
    return jnp.maximum(x, 0.0) + jnp.log1p(jnp.exp(-jnp.abs(x)))


def _rms(x):
    return x * lax.rsqrt(jnp.mean(x * x, axis=-1, keepdims=True) + NORM_EPS)


def _bdot(a, b):
    return jnp.dot(a, b, preferred_element_type=F32)


def _conv3_rows(p, cw, tm):
    rows = p.shape[0]
    prev = pltpu.roll(p, 1, 0)[HALO:HALO + tm]
    nxt = pltpu.roll(p, rows - 1, 0)[HALO:HALO + tm]
    mid = p[HALO:HALO + tm]
    return prev * cw[0:1] + mid * cw[1:2] + nxt * cw[2:3]


def _fill_halo(hx_ref, h_ref, hp_ref, hn_ref, tm, first, last):
    hp = hp_ref[...]
    hn = hn_ref[...]
    hx_ref[0:HALO, :] = jnp.where(first, jnp.zeros_like(hp), hp)
    hx_ref[HALO:HALO + tm, :] = h_ref[...]
    hx_ref[HALO + tm:HALO + tm + HALO, :] = jnp.where(last, jnp.zeros_like(hn), hn)


def _with_halo(h_ref, hp_ref, hn_ref, first, last):
    hp = hp_ref[...]
    hn = hn_ref[...]
    return jnp.concatenate([jnp.where(first, jnp.zeros_like(hp), hp), h_ref[...],
                            jnp.where(last, jnp.zeros_like(hn), hn)], axis=0)


def _halo_specs(tm, t_len, width):
    per = tm // HALO
    nblk = t_len // HALO
    main = pl.BlockSpec((None, tm, width), lambda n, i, *_: (n, i, 0))
    prev = pl.BlockSpec((None, HALO, width), lambda n, i, *_: (n, jnp.maximum(i * per - 1, 0), 0))
    nxt = pl.BlockSpec((None, HALO, width), lambda n, i, *_: (n, jnp.minimum((i + 1) * per, nblk - 1), 0))
    return main, prev, nxt


def _params(sem):
    return pltpu.CompilerParams(dimension_semantics=sem, vmem_limit_bytes=VMEM_LIMIT)


def _prenorm_kernel(x_ref, w_ref, o_ref):
    o_ref[...] = (_rms(x_ref[...]) * w_ref[...]).astype(o_ref.dtype)


def _prenorm(x, w, tm):
    n, t, d = x.shape
    return pl.pallas_call(
        _prenorm_kernel,
        out_shape=jax.ShapeDtypeStruct((n, t, d), BF16),
        grid=(n, t // tm),
        in_specs=[pl.BlockSpec((None, tm, d), lambda a, i: (a, i, 0)),
                  pl.BlockSpec((1, d), lambda a, i: (0, 0))],
        out_specs=pl.BlockSpec((None, tm, d), lambda a, i: (a, i, 0)),
        compiler_params=_params(("parallel", "parallel")),
        name="prenorm",
    )(x, w.reshape(1, d))


QKV_TILE = 512
SC_TILE_GROUPS = 2


def _pipelined(tasks):
    pending = None
    for matmul, epilogue in tasks:
        val = matmul()
        if pending is not None:
            pending[0](pending[1])
        pending = (epilogue, val)
    pending[0](pending[1])


def _split3(x):
    hi = x.astype(BF16)
    r1 = x - hi.astype(F32)
    mid = r1.astype(BF16)
    lo = (r1 - mid.astype(F32)).astype(BF16)
    return hi, mid, lo


def _inproj_kernel(h_ref, hp_ref, hn_ref, wq_ref, wz_ref, ws_ref, cwq_ref, cws_ref, gain_ref, alog_ref, dtb_ref,
                   qkv_ref, z_ref, g_ref, ysc_ref, kt_ref, *, tm):
    i = pl.program_id(1)
    hx = _with_halo(h_ref, hp_ref, hn_ref, i == 0, i == pl.num_programs(1) - 1)
    h = h_ref[...]
    gd = SC_GROUP_DIM
    sc_tile = SC_TILE_GROUPS * 3 * gd

    def halo_matmul(w_ref, c0, width):
        return _bdot(hx, w_ref[:, c0:c0 + width])

    def plain_matmul(w_ref, c0, width):
        return _bdot(h, w_ref[:, c0:c0 + width])

    def gate_epilogue(gl):
        g = -jnp.exp(alog_ref[...]) * _softplus(gl + dtb_ref[...])
        beta = _sigmoid(gl)
        hi, mid, lo = _split3(g)
        r = lax.broadcasted_iota(jnp.int32, (CHUNK, 3 * CHUNK), 0)
        c = lax.broadcasted_iota(jnp.int32, (CHUNK, 3 * CHUNK), 1) % CHUNK
        m_fwd = (c <= r).astype(BF16)
        m_bwd = (c >= r).astype(BF16)
        lane = lax.broadcasted_iota(jnp.int32, (CHUNK, LANES), 1)
        for ci in range(tm // CHUNK):
            rows = slice(ci * CHUNK, (ci + 1) * CHUNK)
            g3 = jnp.concatenate([hi[rows], mid[rows], lo[rows]], axis=0)
            cum_f = _bdot(m_fwd, g3)
            cum_b = _bdot(m_bwd, g3)
            g_ref[rows, :] = jnp.where(lane < DN_HEADS, cum_f,
                                       jnp.where(lane < 2 * DN_HEADS, cum_b, beta[rows]))

    def qkv_epilogue(c0, p):
        y = _silu(_conv3_rows(p, cwq_ref[:, c0:c0 + QKV_TILE], tm))
        if c0 >= 2 * DN_DIM:
            qkv_ref[:, c0:c0 + QKV_TILE] = y.astype(qkv_ref.dtype)
            return
        heads = []
        for hh in range(QKV_TILE // DN_HEAD_DIM):
            yh = y[:, hh * DN_HEAD_DIM:(hh + 1) * DN_HEAD_DIM]
            ss = jnp.sum(yh * yh, axis=-1, keepdims=True)
            sl = slice(c0 + hh * DN_HEAD_DIM, c0 + (hh + 1) * DN_HEAD_DIM)
            heads.append(yh * lax.rsqrt(ss + L2_EPS))
            qkv_ref[:, sl] = heads[-1].astype(qkv_ref.dtype)
        if c0 >= DN_DIM:
            pair0 = (c0 - DN_DIM) // (2 * DN_HEAD_DIM)
            for ci in range(tm // CHUNK):
                rows = slice(ci * CHUNK, (ci + 1) * CHUNK)
                for pp in range(len(heads) // 2):
                    both = jnp.concatenate([heads[2 * pp][rows], heads[2 * pp + 1][rows]], axis=0)
                    kt_ref[ci, pair0 + pp] = both.T.astype(kt_ref.dtype)

    def z_epilogue(c0, p):
        z_ref[:, c0:c0 + QKV_TILE] = _silu(p).astype(z_ref.dtype)

    def sc_epilogue(jn, p):
        for gi in range(SC_TILE_GROUPS):
            base = gi * 3 * gd
            b = p[HALO:HALO + tm, base:base + gd]
            cx = p[:, base + gd:base + 2 * gd] * p[:, base + 2 * gd:base + 3 * gd]
            g0 = (jn * SC_TILE_GROUPS + gi) * gd
            y = b * _conv3_rows(cx, cws_ref[:, g0:g0 + gd], tm)
            ysc_ref[:, g0:g0 + gd] = (_rms(y) * gain_ref[:, g0:g0 + gd]).astype(ysc_ref.dtype)

    part = functools.partial
    tasks = [(part(plain_matmul, wz_ref, DN_DIM, LANES), gate_epilogue)]
    tasks += [(part(halo_matmul, wq_ref, c0, QKV_TILE), part(qkv_epilogue, c0))
              for c0 in range(0, 3 * DN_DIM, QKV_TILE)]
    tasks += [(part(plain_matmul, wz_ref, c0, QKV_TILE), part(z_epilogue, c0))
              for c0 in range(0, DN_DIM, QKV_TILE)]
    tasks += [(part(halo_matmul, ws_ref, jn * sc_tile, sc_tile), part(sc_epilogue, jn))
              for jn in range(SC_GROUPS // SC_TILE_GROUPS)]
    _pipelined(tasks)


def _in_proj(h, w_qkv, w_zg, w_sc, cw_qkv, cw_sc, gain, alog, dtb, tm):
    n, t, d = h.shape
    main, prev, nxt = _halo_specs(tm, t, d)
    whole = lambda shape: pl.BlockSpec(shape, lambda a, i: (0, 0))
    row = lambda width: pl.BlockSpec((None, tm, width), lambda a, i: (a, i, 0))
    return pl.pallas_call(
        functools.partial(_inproj_kernel, tm=tm),
        out_shape=(jax.ShapeDtypeStruct((n, t, 3 * DN_DIM), BF16),
                   jax.ShapeDtypeStruct((n, t, DN_DIM), BF16),
                   jax.ShapeDtypeStruct((n, t, LANES), F32),
                   jax.ShapeDtypeStruct((n, t, SC_DIM), BF16),
                   jax.ShapeDtypeStruct((n, t // CHUNK, DN_HEADS // 2, DN_HEAD_DIM, 2 * CHUNK), BF16)),
        grid=(n, t // tm),
        in_specs=[main, prev, nxt, whole(w_qkv.shape), whole(w_zg.shape), whole(w_sc.shape),
                  whole((3, 3 * DN_DIM)), whole((3, SC_DIM)), whole((1, SC_DIM)),
                  whole((1, LANES)), whole((1, LANES))],
        out_specs=(row(3 * DN_DIM), row(DN_DIM), row(LANES), row(SC_DIM),
                   pl.BlockSpec((None, tm // CHUNK, DN_HEADS // 2, DN_HEAD_DIM, 2 * CHUNK),
                                lambda a, i: (a, i, 0, 0, 0))),
        compiler_params=_params(("parallel", "parallel")),
        name="in_proj",
    )(h, h, h, w_qkv, w_zg, w_sc, cw_qkv, cw_sc, gain.reshape(1, SC_DIM), alog, dtb)


DELTA_UNROLL = 8
GROUP = 2


def _delta_kernel(qf_ref, kf_ref, vf_ref, gcf_ref, grf_ref, ktf_ref, qb_ref, kb_ref, vb_ref, gcb_ref, grb_ref,
                  ktb_ref, of_ref, ob_ref, s_ref, *, tb):
    @pl.when(pl.program_id(1) == 0)
    def _():
        s_ref[...] = jnp.zeros_like(s_ref)

    nch = tb // CHUNK
    unroll = math.gcd(nch, DELTA_UNROLL)
    pair_w = 2 * DN_HEAD_DIM
    r = lax.broadcasted_iota(jnp.int32, (CHUNK, 2 * CHUNK), 0)
    lane = lax.broadcasted_iota(jnp.int32, (CHUNK, 2 * CHUNK), 1)
    c = lane % CHUNK
    second = lane >= CHUNK
    eye = (r == c).astype(F32)
    masks = ((r >= c, r > c), (r <= c, r < c))
    row_head = lambda width: lax.broadcasted_iota(jnp.int32, (2 * CHUNK, width), 0) // CHUNK
    col_head = lambda width, per: lax.broadcasted_iota(jnp.int32, (2 * CHUNK, width), 1) // per
    diag_sq = (row_head(2 * CHUNK) == col_head(2 * CHUNK, CHUNK)).astype(BF16)
    diag_wide = (row_head(pair_w) == col_head(pair_w, DN_HEAD_DIM)).astype(BF16)
    zero_s = jnp.zeros((DN_HEAD_DIM, DN_HEAD_DIM), BF16)
    refs = ((qf_ref, kf_ref, vf_ref, gcf_ref, grf_ref, ktf_ref, of_ref),
            (qb_ref, kb_ref, vb_ref, gcb_ref, grb_ref, ktb_ref, ob_ref))

    def ld(ch, name):
        return ch[name + "_ref"][ch["rows"], ch["sl"]]

    def block_diag(x, sel):
        return jnp.concatenate([x, x], axis=0) * sel

    def pair_cols(x, h0):
        shape = (CHUNK, 2 * CHUNK)
        return jnp.where(second, jnp.broadcast_to(x[:, h0 + 1:h0 + 2], shape), jnp.broadcast_to(x[:, h0:h0 + 1], shape))

    def chains_of(jj):
        chains = []
        for d, (q_ref, k_ref, v_ref, gc_ref, gr_ref, kt_ref, o_ref) in enumerate(refs):
            cc = (nch - 1 - jj) if d else jj
            rows = pl.ds(pl.multiple_of(cc * CHUNK, CHUNK), CHUNK)
            gcol = gc_ref[rows, :]
            grow = gr_ref[cc]
            goff = d * DN_HEADS
            boff = 2 * DN_HEADS + goff
            cum_c = gcol[:, goff:goff + DN_HEADS]
            b_c = gcol[:, boff:boff + DN_HEADS]
            last = 0 if d else CHUNK - 1
            gl_c = cum_c[last:last + 1, :]
            eg_c = jnp.exp(cum_c) * Q_SCALE
            ekd_c = jnp.exp(gl_c - cum_c)
            egl = jnp.exp(gl_c)
            npair = DN_HEADS // 2
            for p in range(npair):
                h0 = 2 * p
                gr2 = grow[d * npair + p:d * npair + p + 1, :]
                br2 = grow[(2 + d) * npair + p:(2 + d) * npair + p + 1, :]
                chains.append(dict(
                    d=d, h0=h0, rows=rows, sl=slice(p * pair_w, (p + 1) * pair_w),
                    q_ref=q_ref, k_ref=k_ref, v_ref=v_ref, o_ref=o_ref, mle=masks[d][0], strict=masks[d][1],
                    kt_ref=kt_ref, cc=cc, p=p,
                    gc=pair_cols(cum_c, h0), bc=pair_cols(b_c, h0), gr=gr2, br=br2, ebr=br2 * jnp.exp(gr2),
                    egc=[eg_c[:, h0 + j:h0 + j + 1] for j in range(2)],
                    ekd=[ekd_c[:, h0 + j:h0 + j + 1] for j in range(2)],
                    egl=[egl[:, h0 + j:h0 + j + 1] for j in range(2)]))
        return chains

    def head(x, j):
        return x[:, j * DN_HEAD_DIM:(j + 1) * DN_HEAD_DIM]

    def bd_sq(x):
        return block_diag(x.astype(BF16), diag_sq)

    def state_free(chains, out):
        decay = [jnp.where(ch["mle"], jnp.exp(jnp.where(ch["mle"], ch["gc"] - ch["gr"], 0.0)), 0.0)
                 for ch in chains]
        bdk = [block_diag(ld(ch, "k"), diag_wide) for ch in chains]
        kq = [lax.dot_general(jnp.concatenate([ld(ch, "k"), ld(ch, "q")], axis=0), x, NT_DIMS,
                              preferred_element_type=F32) for ch, x in zip(chains, bdk)]
        yield
        a_qk = [(x[CHUNK:] * dc * Q_SCALE).astype(BF16) for x, dc in zip(kq, decay)]
        pw = [jnp.where(ch["strict"], -(x[:CHUNK] * dc) * ch["bc"], 0.0) for ch, x, dc in zip(chains, kq, decay)]
        tinv = [eye + p for p in pw]
        pw = [_bdot(p.astype(BF16), bd_sq(p)) for p in pw]
        yield
        for _ in range(4):
            both = [_bdot(jnp.concatenate([t, p], axis=0).astype(BF16), bd_sq(p)) for t, p in zip(tinv, pw)]
            tinv = [t + b[:CHUNK] for t, b in zip(tinv, both)]
            pw = [b[CHUNK:] for b in both]
            yield
        tinv = [t + _bdot(t.astype(BF16), bd_sq(p)) for t, p in zip(tinv, pw)]
        yield
        bdv = [block_diag(ld(ch, "v"), diag_wide) for ch in chains]
        u = [_bdot((t * ch["br"]).astype(BF16), x) for ch, t, x in zip(chains, tinv, bdv)]
        w = [_bdot((t * ch["ebr"]).astype(BF16), x) for ch, t, x in zip(chains, tinv, bdk)]
        qg = [jnp.concatenate([(head(ld(ch, "q"), j).astype(F32) * ch["egc"][j]).astype(BF16) for j in range(2)],
                              axis=1) for ch in chains]
        out.extend(zip(a_qk, u, w, qg))

    def state_step(chains, free, s):
        s_bd = [jnp.concatenate([jnp.concatenate([z[0].astype(BF16), zero_s], axis=1),
                                 jnp.concatenate([zero_s, z[1].astype(BF16)], axis=1)], axis=0) for z in s]
        wq = [_bdot(jnp.concatenate([w.astype(BF16), qg], axis=0), z)
              for (_, _, w, qg), z in zip(free, s_bd)]
        yield
        vn = [u - y[:CHUNK] for (_, u, _, _), y in zip(free, wq)]
        for ch, y, (a_qk, _, _, _), x in zip(chains, wq, free, vn):
            ch["o_ref"][ch["rows"], ch["sl"]] = y[CHUNK:] + _bdot(a_qk, block_diag(x.astype(BF16), diag_wide))
        vd = [jnp.concatenate([(head(x, j) * ch["ekd"][j]).astype(BF16) for j in range(2)], axis=1)
              for ch, x in zip(chains, vn)]
        upd = [_bdot(ch["kt_ref"][ch["cc"], ch["p"]], block_diag(x, diag_wide)) for ch, x in zip(chains, vd)]
        yield
        s[:] = [[z[j] * ch["egl"][j] + head(y, j) for j in range(2)] for ch, z, y in zip(chains, s, upd)]

    def alternate(first, second):
        for _ in first:
            next(second, None)
        for _ in second:
            pass

    def body(it, carry):
        chunks = [chains_of(it * unroll + j) for j in range(unroll)]
        s = [[s_ref[ch["d"], ch["h0"] + j] for j in range(2)] for ch in chunks[0]]
        free = []

        def free_of(lo, hi):
            return state_free([ch for chains in chunks[lo:hi] for ch in chains], free)

        def steps(lo, hi):
            for j in range(lo, hi):
                yield from state_step(chunks[j], free[j * len(chunks[j]):(j + 1) * len(chunks[j])], s)

        bounds = list(range(0, unroll, GROUP)) + [unroll]
        alternate(free_of(bounds[0], bounds[1]), iter(()))
        for g in range(1, len(bounds) - 1):
            alternate(free_of(bounds[g], bounds[g + 1]), steps(bounds[g - 1], bounds[g]))
        alternate(steps(bounds[-2], bounds[-1]), iter(()))
        for ch, z in zip(chunks[0], s):
            for j in range(2):
                s_ref[ch["d"], ch["h0"] + j] = z[j]
        return carry

    lax.fori_loop(0, nch // unroll, body, 0)


def _delta_scan(qkv, gates, gates_t, k_t, tb):
    n, t, _ = qkv.shape
    nb = t // tb
    nch = tb // CHUNK
    fwd = lambda i: i
    bwd = lambda i: nb - 1 - i

    def specs(blk):
        return [pl.BlockSpec((None, tb, DN_DIM), lambda a, i: (a, blk(i), 0)),
                pl.BlockSpec((None, tb, DN_DIM), lambda a, i: (a, blk(i), 1)),
                pl.BlockSpec((None, tb, DN_DIM), lambda a, i: (a, blk(i), 2)),
                pl.BlockSpec((None, tb, LANES), lambda a, i: (a, blk(i), 0)),
                pl.BlockSpec((None, nch, N_GATE_COLS // 2, 2 * CHUNK), lambda a, i: (a, blk(i), 0, 0)),
                pl.BlockSpec((None, nch, DN_HEADS // 2, DN_HEAD_DIM, 2 * CHUNK), lambda a, i: (a, blk(i), 0, 0, 0))]

    out = jax.ShapeDtypeStruct((n, t, DN_DIM), F32)
    args = (qkv, qkv, qkv, gates, gates_t, k_t)
    return pl.pallas_call(
        functools.partial(_delta_kernel, tb=tb),
        out_shape=(out, out),
        grid=(n, nb),
        in_specs=specs(fwd) + specs(bwd),
        out_specs=(pl.BlockSpec((None, tb, DN_DIM), lambda a, i: (a, fwd(i), 0)),
                   pl.BlockSpec((None, tb, DN_DIM), lambda a, i: (a, bwd(i), 0))),
        scratch_shapes=[pltpu.VMEM((2, DN_HEADS, DN_HEAD_DIM, DN_HEAD_DIM), F32)],
        compiler_params=_params(("parallel", "arbitrary")),
        name="delta_scan",
    )(*args, *args)


OUT_ROWS = 256


def _out_kernel(of_ref, ob_ref, zs_ref, ysc_ref, x_ref, w_ref, dnw_ref, npost_ref, nnext_ref,
                xo_ref, hn_ref, odn_ref, *, tm):
    def matmul(r0):
        rows = slice(r0, r0 + OUT_ROWS)
        for hh in range(DN_HEADS):
            sl = slice(hh * DN_HEAD_DIM, (hh + 1) * DN_HEAD_DIM)
            o = of_ref[rows, sl] + ob_ref[rows, sl]
            y = _rms(o) * dnw_ref[...] * zs_ref[rows, sl].astype(F32)
            odn_ref[rows, sl] = y.astype(odn_ref.dtype)
        return _bdot(odn_ref[rows, :], w_ref[0:DN_DIM, :]) + _bdot(ysc_ref[rows, :], w_ref[DN_DIM:, :])

    def epilogue(r0, m):
        rows = slice(r0, r0 + OUT_ROWS)
        xn = x_ref[rows, :] + _rms(m) * npost_ref[...]
        xo_ref[rows, :] = xn
        hn_ref[rows, :] = (_rms(xn) * nnext_ref[...]).astype(hn_ref.dtype)

    _pipelined([(functools.partial(matmul, r0), functools.partial(epilogue, r0))
                for r0 in range(0, tm, OUT_ROWS)])


def _out_proj(o_f, o_b, zs, ysc, x, w, dn_w, n_post, n_next, tm):
    n, t, d = x.shape
    assert tm % OUT_ROWS == 0
    row = lambda width: pl.BlockSpec((None, tm, width), lambda a, i: (a, i, 0))
    vec = lambda width: pl.BlockSpec((1, width), lambda a, i: (0, 0))
    return pl.pallas_call(
        functools.partial(_out_kernel, tm=tm),
        out_shape=(jax.ShapeDtypeStruct((n, t, d), F32), jax.ShapeDtypeStruct((n, t, d), BF16)),
        grid=(n, t // tm),
        in_specs=[row(DN_DIM), row(DN_DIM), row(DN_DIM), row(SC_DIM), row(d),
                  pl.BlockSpec((d, d), lambda a, i: (0, 0)),
                  vec(DN_HEAD_DIM), vec(d), vec(d)],
        out_specs=(row(d), row(d)),
        scratch_shapes=[pltpu.VMEM((tm, DN_DIM), BF16)],
        compiler_params=_params(("parallel", "parallel")),
        name="out_proj",
    )(o_f, o_b, zs, ysc, x, w, dn_w.reshape(1, -1), n_post.reshape(1, d), n_next.reshape(1, d))


def _ffn_kernel(h_ref, hp_ref, hn_ref, wa_ref, wb_ref, cw_ref, wd_ref, x_ref, npost_ref, nnext_ref,
                xo_ref, *rest, tm, emit_next):
    ho_ref = rest[0] if emit_next else None
    hx_ref, acc_ref = rest[-2:]
    i = pl.program_id(1)
    f = pl.program_id(2)

    @pl.when(f == 0)
    def _():
        _fill_halo(hx_ref, h_ref, hp_ref, hn_ref, tm, i == 0, i == pl.num_programs(1) - 1)
        acc_ref[...] = jnp.zeros_like(acc_ref)

    a = _bdot(hx_ref[...], wa_ref[...])
    b = _bdot(h_ref[...], wb_ref[...])
    act = (_silu(_conv3_rows(a, cw_ref[...], tm)) * b).astype(BF16)
    acc_ref[...] += _bdot(act, wd_ref[...])

    @pl.when(f == pl.num_programs(2) - 1)
    def _():
        xn = x_ref[...] + _rms(acc_ref[...]) * npost_ref[...]
        xo_ref[...] = xn
        if emit_next:
            ho_ref[...] = (_rms(xn) * nnext_ref[...]).astype(ho_ref.dtype)


def _ffn(h, x, w_up, cw, w_down, n_post, n_next, emit_next, tm, tf=512):
    n, t, d = x.shape
    nf = D_FF // tf
    main, prev, nxt = _halo_specs(tm, t, d)
    row = lambda: pl.BlockSpec((None, tm, d), lambda a, i, f: (a, i, 0))
    vec = lambda: pl.BlockSpec((1, d), lambda a, i, f: (0, 0))
    outs = (jax.ShapeDtypeStruct((n, t, d), F32), jax.ShapeDtypeStruct((n, t, d), BF16))[:2 if emit_next else 1]
    return pl.pallas_call(
        functools.partial(_ffn_kernel, tm=tm, emit_next=emit_next),
        out_shape=outs,
        grid=(n, t // tm, nf),
        in_specs=[main, prev, nxt,
                  pl.BlockSpec((d, tf), lambda a, i, f: (0, f)),
                  pl.BlockSpec((d, tf), lambda a, i, f: (0, f + nf)),
                  pl.BlockSpec((3, tf), lambda a, i, f: (0, f)),
                  pl.BlockSpec((tf, d), lambda a, i, f: (f, 0)),
                  row(), vec(), vec()],
        out_specs=tuple(row() for _ in outs),
        scratch_shapes=[pltpu.VMEM((tm + 2 * HALO, d), BF16), pltpu.VMEM((tm, d), F32)],
        compiler_params=_params(("parallel", "parallel", "arbitrary")),
        name="ffn",
    )(h, h, h, w_up, w_up, cw, w_down, x, n_post.reshape(1, d), n_next.reshape(1, d))


def _tiles(t):
    return min(512, t), min(DELTA_UNROLL * CHUNK, t)


def _prepare(w_in, a_log, dt_bias):
    depth = w_in.shape[0]
    w_qkv = w_in[:, :, :3 * DN_DIM].astype(BF16)
    z_end = 4 * DN_DIM
    g_end = z_end + N_GATE_COLS
    pad = jnp.zeros((depth, D_MODEL, LANES - N_GATE_COLS), w_in.dtype)
    w_zg = jnp.concatenate([w_in[:, :, 3 * DN_DIM:g_end], pad], axis=-1).astype(BF16)
    w_sc = w_in[:, :, g_end:].reshape(depth, D_MODEL, 3, SC_GROUPS, SC_GROUP_DIM)
    w_sc = w_sc.transpose(0, 1, 3, 2, 4).reshape(depth, D_MODEL, 3 * SC_DIM).astype(BF16)
    vpad = jnp.zeros((depth, 1, LANES - 2 * DN_HEADS), F32)
    alog = jnp.concatenate([a_log.reshape(depth, 1, 2 * DN_HEADS).astype(F32), vpad], axis=-1)
    dtb = jnp.concatenate([dt_bias.reshape(depth, 1, 2 * DN_HEADS).astype(F32), vpad], axis=-1)
    return w_qkv, w_zg, w_sc, alog, dtb


def _trunk(x, prm):
    (norm_mix_pre, w_qkv, w_zg, w_sc, alog, dtb, conv_qkv, dn_norm, conv_sc, sc_norm, w_out,
     norm_mix_post, norm_ffn_pre, w_up, conv_ffn, w_down, norm_ffn_post) = prm
    depth = w_qkv.shape[0]
    n, t, _ = x.shape
    tm, tb = _tiles(t)
    h = _prenorm(x, norm_mix_pre[0], tm)
    for l in range(depth):
        qkv, zs, gates, ysc, k_t = _in_proj(h, w_qkv[l], w_zg[l], w_sc[l], conv_qkv[l], conv_sc[l], sc_norm[l],
                                            alog[l], dtb[l], tm)
        gates_t = gates[:, :, :N_GATE_COLS].reshape(n, t // CHUNK, CHUNK, 4, DN_HEADS // 2, 2)
        gates_t = gates_t.transpose(0, 1, 3, 4, 5, 2).reshape(n, t // CHUNK, N_GATE_COLS // 2, 2 * CHUNK)
        o_f, o_b = _delta_scan(qkv, gates, gates_t, k_t, tb)
        x, h = _out_proj(o_f, o_b, zs, ysc, x, w_out[l], dn_norm[l], norm_mix_post[l], norm_ffn_pre[l], tm)
        last = l == depth - 1
        x, *h = _ffn(h, x, w_up[l], conv_ffn[l], w_down[l], norm_ffn_post[l],
                     norm_mix_pre[0 if last else l + 1], not last, tm)
        h = h[0] if h else None
    return x


def kernel(x_prompt, x_sample, norm_mix_pre, w_in, conv_qkv, a_log, dt_bias, dn_norm, conv_sc, sc_norm, w_out, norm_mix_post, norm_ffn_pre, w_up, conv_ffn, w_down, norm_ffn_post):
    w_qkv, w_zg, w_sc, alog, dtb = _prepare(w_in, a_log, dt_bias)
    prm = (norm_mix_pre, w_qkv, w_zg, w_sc, alog, dtb, conv_qkv, dn_norm, conv_sc, sc_norm,
           w_out.astype(BF16), norm_mix_post, norm_ffn_pre, w_up.astype(BF16), conv_ffn,
           w_down.astype(BF16), norm_ffn_post)
    return (_trunk(x_prompt, prm), _trunk(x_sample, prm))
```

```python
import functools
import math

import jax
import jax.numpy as jnp
from jax import lax
from jax.experimental import pallas as pl
from jax.experimental.pallas import tpu as pltpu

D_MODEL = 2048
DN_HEADS = 8
DN_HEAD_DIM = 128
DN_DIM = DN_HEADS * DN_HEAD_DIM
SC_GROUPS = 8
SC_GROUP_DIM = 128
SC_DIM = SC_GROUPS * SC_GROUP_DIM
D_FF = 5632
CHUNK = 64
N_GATE_COLS = 4 * DN_HEADS
NORM_EPS = 1e-6
L2_EPS = 1e-6
Q_SCALE = DN_HEAD_DIM ** -0.5

LANES = 128
HALO = 16
VMEM_LIMIT = 56 * 1024 * 1024

F32 = jnp.float32
BF16 = jnp.bfloat16
NT_DIMS = (((1,), (1,)), ((), ()))
TN_DIMS = (((0,), (0,)), ((), ()))


def _sigmoid(x):
    return 1.0 / (1.0 + jnp.exp(-x))


def _silu(x):
    return x * _sigmoid(x)


def _softplus(x):
    return jnp.maximum(x, 0.0) + jnp.log1p(jnp.exp(-jnp.abs(x)))


def _rms(x):
    return x * lax.rsqrt(jnp.mean(x * x, axis=-1, keepdims=True) + NORM_EPS)


def _bdot(a, b):
    return jnp.dot(a, b, preferred_element_type=F32)


def _conv3_rows(p, cw, tm):
    rows = p.shape[0]
    prev = pltpu.roll(p, 1, 0)[HALO:HALO + tm]
    nxt = pltpu.roll(p, rows - 1, 0)[HALO:HALO + tm]
    mid = p[HALO:HALO + tm]
    return prev * cw[0:1] + mid * cw[1:2] + nxt * cw[2:3]


def _fill_halo(hx_ref, h_ref, hp_ref, hn_ref, tm, first, last):
    hp = hp_ref[...]
    hn = hn_ref[...]
    hx_ref[0:HALO, :] = jnp.where(first, jnp.zeros_like(hp), hp)
    hx_ref[HALO:HALO + tm, :] = h_ref[...]
    hx_ref[HALO + tm:HALO + tm + HALO, :] = jnp.where(last, jnp.zeros_like(hn), hn)


def _with_halo(h_ref, hp_ref, hn_ref, first, last):
    hp = hp_ref[...]
    hn = hn_ref[...]
    return jnp.concatenate([jnp.where(first, jnp.zeros_like(hp), hp), h_ref[...],
                            jnp.where(last, jnp.zeros_like(hn), hn)], axis=0)


def _halo_specs(tm, t_len, width):
    per = tm // HALO
    nblk = t_len // HALO
    main = pl.BlockSpec((None, tm, width), lambda n, i, *_: (n, i, 0))
    prev = pl.BlockSpec((None, HALO, width), lambda n, i, *_: (n, jnp.maximum(i * per - 1, 0), 0))
    nxt = pl.BlockSpec((None, HALO, width), lambda n, i, *_: (n, jnp.minimum((i + 1) * per, nblk - 1), 0))
    return main, prev, nxt


def _params(sem):
    return pltpu.CompilerParams(dimension_semantics=sem, vmem_limit_bytes=VMEM_LIMIT)


def _prenorm_kernel(x_ref, w_ref, o_ref):
    o_ref[...] = (_rms(x_ref[...]) * w_ref[...]).astype(o_ref.dtype)


def _prenorm(x, w, tm):
    n, t, d = x.shape
    return pl.pallas_call(
        _prenorm_kernel,
        out_shape=jax.ShapeDtypeStruct((n, t, d), BF16),
        grid=(n, t // tm),
        in_specs=[pl.BlockSpec((None, tm, d), lambda a, i: (a, i, 0)),
                  pl.BlockSpec((1, d), lambda a, i: (0, 0))],
        out_specs=pl.BlockSpec((None, tm, d), lambda a, i: (a, i, 0)),
        compiler_params=_params(("parallel", "parallel")),
        name="prenorm",
    )(x, w.reshape(1, d))


QKV_TILE = 512
SC_TILE_GROUPS = 2


def _pipelined(tasks):
    pending = None
    for matmul, epilogue in tasks:
        val = matmul()
        if pending is not None:
            pending[0](pending[1])
        pending = (epilogue, val)
    pending[0](pending[1])


def _split3(x):
    hi = x.astype(BF16)
    r1 = x - hi.astype(F32)
    mid = r1.astype(BF16)
    lo = (r1 - mid.astype(F32)).astype(BF16)
    return hi, mid, lo


def _inproj_kernel(h_ref, hp_ref, hn_ref, wq_ref, wz_ref, ws_ref, cwq_ref, cws_ref, gain_ref, alog_ref, dtb_ref,
                   qkv_ref, z_ref, g_ref, ysc_ref, kt_ref, *, tm):
    i = pl.program_id(1)
    hx = _with_halo(h_ref, hp_ref, hn_ref, i == 0, i == pl.num_programs(1) - 1)
    h = h_ref[...]
    gd = SC_GROUP_DIM
    sc_tile = SC_TILE_GROUPS * 3 * gd

    def halo_matmul(w_ref, c0, width):
        return _bdot(hx, w_ref[:, c0:c0 + width])

    def plain_matmul(w_ref, c0, width):
        return _bdot(h, w_ref[:, c0:c0 + width])

    def gate_epilogue(gl):
        g = -jnp.exp(alog_ref[...]) * _softplus(gl + dtb_ref[...])
        beta = _sigmoid(gl)
        hi, mid, lo = _split3(g)
        r = lax.broadcasted_iota(jnp.int32, (CHUNK, 3 * CHUNK), 0)
        c = lax.broadcasted_iota(jnp.int32, (CHUNK, 3 * CHUNK), 1) % CHUNK
        m_fwd = (c <= r).astype(BF16)
        m_bwd = (c >= r).astype(BF16)
        lane = lax.broadcasted_iota(jnp.int32, (CHUNK, LANES), 1)
        for ci in range(tm // CHUNK):
            rows = slice(ci * CHUNK, (ci + 1) * CHUNK)
            g3 = jnp.concatenate([hi[rows], mid[rows], lo[rows]], axis=0)
            cum_f = _bdot(m_fwd, g3)
            cum_b = _bdot(m_bwd, g3)
            g_ref[rows, :] = jnp.where(lane < DN_HEADS, cum_f,
                                       jnp.where(lane < 2 * DN_HEADS, cum_b, beta[rows]))

    def qkv_epilogue(c0, p):
        y = _silu(_conv3_rows(p, cwq_ref[:, c0:c0 + QKV_TILE], tm))
        if c0 >= 2 * DN_DIM:
            qkv_ref[:, c0:c0 + QKV_TILE] = y.astype(qkv_ref.dtype)
            return
        heads = []
        for hh in range(QKV_TILE // DN_HEAD_DIM):
            yh = y[:, hh * DN_HEAD_DIM:(hh + 1) * DN_HEAD_DIM]
            ss = jnp.sum(yh * yh, axis=-1, keepdims=True)
            sl = slice(c0 + hh * DN_HEAD_DIM, c0 + (hh + 1) * DN_HEAD_DIM)
            heads.append(yh * lax.rsqrt(ss + L2_EPS))
            qkv_ref[:, sl] = heads[-1].astype(qkv_ref.dtype)
        if c0 >= DN_DIM:
            pair0 = (c0 - DN_DIM) // (2 * DN_HEAD_DIM)
            for ci in range(tm // CHUNK):
                rows = slice(ci * CHUNK, (ci + 1) * CHUNK)
                for pp in range(len(heads) // 2):
                    both = jnp.concatenate([heads[2 * pp][rows], heads[2 * pp + 1][rows]], axis=0)
                    kt_ref[ci, pair0 + pp] = both.T.astype(kt_ref.dtype)

    def z_epilogue(c0, p):
        z_ref[:, c0:c0 + p.shape[1]] = _silu(p).astype(z_ref.dtype)

    def sc_epilogue(jn, p):
        for gi in range(SC_TILE_GROUPS):
            base = gi * 3 * gd
            b = p[HALO:HALO + tm, base:base + gd]
            cx = p[:, base + gd:base + 2 * gd] * p[:, base + 2 * gd:base + 3 * gd]
            g0 = (jn * SC_TILE_GROUPS + gi) * gd
            y = b * _conv3_rows(cx, cws_ref[:, g0:g0 + gd], tm)
            ysc_ref[:, g0:g0 + gd] = (_rms(y) * gain_ref[:, g0:g0 + gd]).astype(ysc_ref.dtype)

    part = functools.partial
    zq = QKV_TILE // 2

    def z_gate_epilogue(p):
        z_epilogue(DN_DIM - zq, p[:, :zq])
        gate_epilogue(p[:, zq:])

    tasks = [(part(plain_matmul, wz_ref, DN_DIM - zq, zq + LANES), z_gate_epilogue)]
    tasks += [(part(halo_matmul, wq_ref, c0, QKV_TILE), part(qkv_epilogue, c0))
              for c0 in range(0, 3 * DN_DIM, QKV_TILE)]
    tasks += [(part(plain_matmul, wz_ref, DN_DIM - 2 * zq, zq), part(z_epilogue, DN_DIM - 2 * zq))]
    tasks += [(part(halo_matmul, ws_ref, jn * sc_tile, sc_tile), part(sc_epilogue, jn))
              for jn in range(SC_GROUPS // SC_TILE_GROUPS)]
    tasks += [(part(plain_matmul, wz_ref, c0, QKV_TILE), part(z_epilogue, c0))
              for c0 in range(0, DN_DIM - 2 * zq, QKV_TILE)]
    _pipelined(tasks)


def _in_proj(h, w_qkv, w_zg, w_sc, cw_qkv, cw_sc, gain, alog, dtb, tm):
    n, t, d = h.shape
    main, prev, nxt = _halo_specs(tm, t, d)
    whole = lambda shape: pl.BlockSpec(shape, lambda a, i: (0, 0))
    row = lambda width: pl.BlockSpec((None, tm, width), lambda a, i: (a, i, 0))
    return pl.pallas_call(
        functools.partial(_inproj_kernel, tm=tm),
        out_shape=(jax.ShapeDtypeStruct((n, t, 3 * DN_DIM), BF16),
                   jax.ShapeDtypeStruct((n, t, DN_DIM), BF16),
                   jax.ShapeDtypeStruct((n, t, LANES), F32),
                   jax.ShapeDtypeStruct((n, t, SC_DIM), BF16),
                   jax.ShapeDtypeStruct((n, t // CHUNK, DN_HEADS // 2, DN_HEAD_DIM, 2 * CHUNK), BF16)),
        grid=(n, t // tm),
        in_specs=[main, prev, nxt, whole(w_qkv.shape), whole(w_zg.shape), whole(w_sc.shape),
                  whole((3, 3 * DN_DIM)), whole((3, SC_DIM)), whole((1, SC_DIM)),
                  whole((1, LANES)), whole((1, LANES))],
        out_specs=(row(3 * DN_DIM), row(DN_DIM), row(LANES), row(SC_DIM),
                   pl.BlockSpec((None, tm // CHUNK, DN_HEADS // 2, DN_HEAD_DIM, 2 * CHUNK),
                                lambda a, i: (a, i, 0, 0, 0))),
        compiler_params=_params(("parallel", "parallel")),
        name="in_proj",
    )(h, h, h, w_qkv, w_zg, w_sc, cw_qkv, cw_sc, gain.reshape(1, SC_DIM), alog, dtb)


DELTA_UNROLL = 8
GROUP = 2


def _delta_kernel(qf_ref, kf_ref, vf_ref, gcf_ref, grf_ref, ktf_ref, qb_ref, kb_ref, vb_ref, gcb_ref, grb_ref,
                  ktb_ref, of_ref, ob_ref, s_ref, *, tb):
    @pl.when(pl.program_id(1) == 0)
    def _():
        s_ref[...] = jnp.zeros_like(s_ref)

    nch = tb // CHUNK
    unroll = math.gcd(nch, DELTA_UNROLL)
    pair_w = 2 * DN_HEAD_DIM
    r = lax.broadcasted_iota(jnp.int32, (CHUNK, 2 * CHUNK), 0)
    lane = lax.broadcasted_iota(jnp.int32, (CHUNK, 2 * CHUNK), 1)
    c = lane % CHUNK
    second = lane >= CHUNK
    eye = (r == c).astype(F32)
    masks = ((r >= c, r > c), (r <= c, r < c))
    row_head = lambda width: lax.broadcasted_iota(jnp.int32, (2 * CHUNK, width), 0) // CHUNK
    col_head = lambda width, per: lax.broadcasted_iota(jnp.int32, (2 * CHUNK, width), 1) // per
    diag_sq = (row_head(2 * CHUNK) == col_head(2 * CHUNK, CHUNK)).astype(BF16)
    diag_wide = (row_head(pair_w) == col_head(pair_w, DN_HEAD_DIM)).astype(BF16)
    zero_s = jnp.zeros((DN_HEAD_DIM, DN_HEAD_DIM), BF16)
    refs = ((qf_ref, kf_ref, vf_ref, gcf_ref, grf_ref, ktf_ref, of_ref),
            (qb_ref, kb_ref, vb_ref, gcb_ref, grb_ref, ktb_ref, ob_ref))

    def ld(ch, name):
        return ch[name + "_ref"][ch["rows"], ch["sl"]]

    def block_diag(x, sel):
        return jnp.concatenate([x, x], axis=0) * sel

    def pair_cols(x, h0):
        shape = (CHUNK, 2 * CHUNK)
        return jnp.where(second, jnp.broadcast_to(x[:, h0 + 1:h0 + 2], shape), jnp.broadcast_to(x[:, h0:h0 + 1], shape))

    def chains_of(jj):
        chains = []
        for d, (q_ref, k_ref, v_ref, gc_ref, gr_ref, kt_ref, o_ref) in enumerate(refs):
            cc = (nch - 1 - jj) if d else jj
            rows = pl.ds(pl.multiple_of(cc * CHUNK, CHUNK), CHUNK)
            gcol = gc_ref[rows, :]
            grow = gr_ref[cc]
            goff = d * DN_HEADS
            boff = 2 * DN_HEADS + goff
            cum_c = gcol[:, goff:goff + DN_HEADS]
            b_c = gcol[:, boff:boff + DN_HEADS]
            last = 0 if d else CHUNK - 1
            gl_c = cum_c[last:last + 1, :]
            eg_c = jnp.exp(cum_c) * Q_SCALE
            ekd_c = jnp.exp(gl_c - cum_c)
            egl = jnp.exp(gl_c)
            npair = DN_HEADS // 2
            for p in range(npair):
                h0 = 2 * p
                gr2 = grow[d * npair + p:d * npair + p + 1, :]
                br2 = grow[(2 + d) * npair + p:(2 + d) * npair + p + 1, :]
                chains.append(dict(
                    d=d, h0=h0, rows=rows, sl=slice(p * pair_w, (p + 1) * pair_w),
                    q_ref=q_ref, k_ref=k_ref, v_ref=v_ref, o_ref=o_ref, mle=masks[d][0], strict=masks[d][1],
                    kt_ref=kt_ref, cc=cc, p=p,
                    gc=pair_cols(cum_c, h0), bc=pair_cols(b_c, h0), gr=gr2, br=br2, ebr=br2 * jnp.exp(gr2),
                    egc=[eg_c[:, h0 + j:h0 + j + 1] for j in range(2)],
                    ekd=[ekd_c[:, h0 + j:h0 + j + 1] for j in range(2)],
                    egl=[egl[:, h0 + j:h0 + j + 1] for j in range(2)]))
        return chains

    def head(x, j):
        return x[:, j * DN_HEAD_DIM:(j + 1) * DN_HEAD_DIM]

    def bd_sq(x):
        return block_diag(x.astype(BF16), diag_sq)

    def state_free(chains, out):
        decay = [jnp.where(ch["mle"], jnp.exp(jnp.where(ch["mle"], ch["gc"] - ch["gr"], 0.0)), 0.0)
                 for ch in chains]
        bdk = [block_diag(ld(ch, "k"), diag_wide) for ch in chains]
        kq = [lax.dot_general(jnp.concatenate([ld(ch, "k"), ld(ch, "q")], axis=0), x, NT_DIMS,
                              preferred_element_type=F32) for ch, x in zip(chains, bdk)]
        yield
        a_qk = [(x[CHUNK:] * dc * Q_SCALE).astype(BF16) for x, dc in zip(kq, decay)]
        pw = [jnp.where(ch["strict"], -(x[:CHUNK] * dc) * ch["bc"], 0.0) for ch, x, dc in zip(chains, kq, decay)]
        tinv = [eye + p for p in pw]
        pw = [_bdot(p.astype(BF16), bd_sq(p)) for p in pw]
        yield
        for _ in range(4):
            both = [_bdot(jnp.concatenate([t, p], axis=0).astype(BF16), bd_sq(p)) for t, p in zip(tinv, pw)]
            tinv = [t + b[:CHUNK] for t, b in zip(tinv, both)]
            pw = [b[CHUNK:] for b in both]
            yield
        tinv = [t + _bdot(t.astype(BF16), bd_sq(p)) for t, p in zip(tinv, pw)]
        yield
        bdv = [block_diag(ld(ch, "v"), diag_wide) for ch in chains]
        u = [_bdot((t * ch["br"]).astype(BF16), x) for ch, t, x in zip(chains, tinv, bdv)]
        w = [_bdot((t * ch["ebr"]).astype(BF16), x) for ch, t, x in zip(chains, tinv, bdk)]
        qg = [jnp.concatenate([(head(ld(ch, "q"), j).astype(F32) * ch["egc"][j]).astype(BF16) for j in range(2)],
                              axis=1) for ch in chains]
        out.extend(zip(a_qk, u, w, qg))

    def state_step(chains, free, s):
        s_bd = [jnp.concatenate([jnp.concatenate([z[0].astype(BF16), zero_s], axis=1),
                                 jnp.concatenate([zero_s, z[1].astype(BF16)], axis=1)], axis=0) for z in s]
        wq = [_bdot(jnp.concatenate([w.astype(BF16), qg], axis=0), z)
              for (_, _, w, qg), z in zip(free, s_bd)]
        yield
        vn = [u - y[:CHUNK] for (_, u, _, _), y in zip(free, wq)]
        for ch, y, (a_qk, _, _, _), x in zip(chains, wq, free, vn):
            ch["o_ref"][ch["rows"], ch["sl"]] = y[CHUNK:] + _bdot(a_qk, block_diag(x.astype(BF16), diag_wide))
        vd = [jnp.concatenate([(head(x, j) * ch["ekd"][j]).astype(BF16) for j in range(2)], axis=1)
              for ch, x in zip(chains, vn)]
        upd = [_bdot(ch["kt_ref"][ch["cc"], ch["p"]], block_diag(x, diag_wide)) for ch, x in zip(chains, vd)]
        yield
        s[:] = [[z[j] * ch["egl"][j] + head(y, j) for j in range(2)] for ch, z, y in zip(chains, s, upd)]

    def alternate(first, second):
        for _ in first:
            next(second, None)
        for _ in second:
            pass

    def body(it, carry):
        chunks = [chains_of(it * unroll + j) for j in range(unroll)]
        s = [[s_ref[ch["d"], ch["h0"] + j] for j in range(2)] for ch in chunks[0]]
        free = []

        def free_of(lo, hi):
            return state_free([ch for chains in chunks[lo:hi] for ch in chains], free)

        def steps(lo, hi):
            for j in range(lo, hi):
                yield from state_step(chunks[j], free[j * len(chunks[j]):(j + 1) * len(chunks[j])], s)

        bounds = list(range(0, unroll, GROUP)) + [unroll]
        alternate(free_of(bounds[0], bounds[1]), iter(()))
        for g in range(1, len(bounds) - 1):
            alternate(free_of(bounds[g], bounds[g + 1]), steps(bounds[g - 1], bounds[g]))
        alternate(steps(bounds[-2], bounds[-1]), iter(()))
        for ch, z in zip(chunks[0], s):
            for j in range(2):
                s_ref[ch["d"], ch["h0"] + j] = z[j]
        return carry

    lax.fori_loop(0, nch // unroll, body, 0)


def _delta_scan(qkv, gates, gates_t, k_t, tb):
    n, t, _ = qkv.shape
    nb = t // tb
    nch = tb // CHUNK
    fwd = lambda i: i
    bwd = lambda i: nb - 1 - i

    def specs(blk):
        return [pl.BlockSpec((None, tb, DN_DIM), lambda a, i: (a, blk(i), 0)),
                pl.BlockSpec((None, tb, DN_DIM), lambda a, i: (a, blk(i), 1)),
                pl.BlockSpec((None, tb, DN_DIM), lambda a, i: (a, blk(i), 2)),
                pl.BlockSpec((None, tb, LANES), lambda a, i: (a, blk(i), 0)),
                pl.BlockSpec((None, nch, N_GATE_COLS // 2, 2 * CHUNK), lambda a, i: (a, blk(i), 0, 0)),
                pl.BlockSpec((None, nch, DN_HEADS // 2, DN_HEAD_DIM, 2 * CHUNK), lambda a, i: (a, blk(i), 0, 0, 0))]

    out = jax.ShapeDtypeStruct((n, t, DN_DIM), F32)
    args = (qkv, qkv, qkv, gates, gates_t, k_t)
    return pl.pallas_call(
        functools.partial(_delta_kernel, tb=tb),
        out_shape=(out, out),
        grid=(n, nb),
        in_specs=specs(fwd) + specs(bwd),
        out_specs=(pl.BlockSpec((None, tb, DN_DIM), lambda a, i: (a, fwd(i), 0)),
                   pl.BlockSpec((None, tb, DN_DIM), lambda a, i: (a, bwd(i), 0))),
        scratch_shapes=[pltpu.VMEM((2, DN_HEADS, DN_HEAD_DIM, DN_HEAD_DIM), F32)],
        compiler_params=_params(("parallel", "arbitrary")),
        name="delta_scan",
    )(*args, *args)


OUT_ROWS = 256


def _out_kernel(of_ref, ob_ref, zs_ref, ysc_ref, x_ref, w_ref, dnw_ref, npost_ref, nnext_ref,
                xo_ref, hn_ref, odn_ref, *, tm):
    def matmul(r0):
        rows = slice(r0, r0 + OUT_ROWS)
        for hh in range(DN_HEADS):
            sl = slice(hh * DN_HEAD_DIM, (hh + 1) * DN_HEAD_DIM)
            o = of_ref[rows, sl] + ob_ref[rows, sl]
            y = _rms(o) * dnw_ref[...] * zs_ref[rows, sl].astype(F32)
            odn_ref[rows, sl] = y.astype(odn_ref.dtype)
        return _bdot(odn_ref[rows, :], w_ref[0:DN_DIM, :]) + _bdot(ysc_ref[rows, :], w_ref[DN_DIM:, :])

    def epilogue(r0, m):
        rows = slice(r0, r0 + OUT_ROWS)
        xn = x_ref[rows, :] + _rms(m) * npost_ref[...]
        xo_ref[rows, :] = xn
        hn_ref[rows, :] = (_rms(xn) * nnext_ref[...]).astype(hn_ref.dtype)

    _pipelined([(functools.partial(matmul, r0), functools.partial(epilogue, r0))
                for r0 in range(0, tm, OUT_ROWS)])


def _out_proj(o_f, o_b, zs, ysc, x, w, dn_w, n_post, n_next, tm):
    n, t, d = x.shape
    assert tm % OUT_ROWS == 0
    row = lambda width: pl.BlockSpec((None, tm, width), lambda a, i: (a, i, 0))
    vec = lambda width: pl.BlockSpec((1, width), lambda a, i: (0, 0))
    return pl.pallas_call(
        functools.partial(_out_kernel, tm=tm),
        out_shape=(jax.ShapeDtypeStruct((n, t, d), F32), jax.ShapeDtypeStruct((n, t, d), BF16)),
        grid=(n, t // tm),
        in_specs=[row(DN_DIM), row(DN_DIM), row(DN_DIM), row(SC_DIM), row(d),
                  pl.BlockSpec((d, d), lambda a, i: (0, 0)),
                  vec(DN_HEAD_DIM), vec(d), vec(d)],
        out_specs=(row(d), row(d)),
        scratch_shapes=[pltpu.VMEM((tm, DN_DIM), BF16)],
        compiler_params=_params(("parallel", "parallel")),
        name="out_proj",
    )(o_f, o_b, zs, ysc, x, w, dn_w.reshape(1, -1), n_post.reshape(1, d), n_next.reshape(1, d))


def _ffn_kernel(h_ref, hp_ref, hn_ref, wa_ref, wb_ref, cw_ref, wd_ref, x_ref, npost_ref, nnext_ref,
                xo_ref, *rest, tm, emit_next):
    ho_ref = rest[0] if emit_next else None
    hx_ref, acc_ref = rest[-2:]
    i = pl.program_id(1)
    f = pl.program_id(2)

    @pl.when(f == 0)
    def _():
        _fill_halo(hx_ref, h_ref, hp_ref, hn_ref, tm, i == 0, i == pl.num_programs(1) - 1)
        acc_ref[...] = jnp.zeros_like(acc_ref)

    a = _bdot(hx_ref[...], wa_ref[...])
    b = _bdot(h_ref[...], wb_ref[...])
    act = (_silu(_conv3_rows(a, cw_ref[...], tm)) * b).astype(BF16)
    acc_ref[...] += _bdot(act, wd_ref[...])

    @pl.when(f == pl.num_programs(2) - 1)
    def _():
        xn = x_ref[...] + _rms(acc_ref[...]) * npost_ref[...]
        xo_ref[...] = xn
        if emit_next:
            ho_ref[...] = (_rms(xn) * nnext_ref[...]).astype(ho_ref.dtype)


def _ffn(h, x, w_up, cw, w_down, n_post, n_next, emit_next, tm, tf=512):
    n, t, d = x.shape
    nf = D_FF // tf
    main, prev, nxt = _halo_specs(tm, t, d)
    row = lambda: pl.BlockSpec((None, tm, d), lambda a, i, f: (a, i, 0))
    vec = lambda: pl.BlockSpec((1, d), lambda a, i, f: (0, 0))
    outs = (jax.ShapeDtypeStruct((n, t, d), F32), jax.ShapeDtypeStruct((n, t, d), BF16))[:2 if emit_next else 1]
    return pl.pallas_call(
        functools.partial(_ffn_kernel, tm=tm, emit_next=emit_next),
        out_shape=outs,
        grid=(n, t // tm, nf),
        in_specs=[main, prev, nxt,
                  pl.BlockSpec((d, tf), lambda a, i, f: (0, f)),
                  pl.BlockSpec((d, tf), lambda a, i, f: (0, f + nf)),
                  pl.BlockSpec((3, tf), lambda a, i, f: (0, f)),
                  pl.BlockSpec((tf, d), lambda a, i, f: (f, 0)),
                  row(), vec(), vec()],
        out_specs=tuple(row() for _ in outs),
        scratch_shapes=[pltpu.VMEM((tm + 2 * HALO, d), BF16), pltpu.VMEM((tm, d), F32)],
        compiler_params=_params(("parallel", "parallel", "arbitrary")),
        name="ffn",
    )(h, h, h, w_up, w_up, cw, w_down, x, n_post.reshape(1, d), n_next.reshape(1, d))


def _tiles(t):
    return min(512, t), min(DELTA_UNROLL * CHUNK, t)


def _win_kernel(w_ref, q_ref, zg_ref, sc_ref):
    z_end = 4 * DN_DIM
    g_end = z_end + N_GATE_COLS
    q_ref[...] = w_ref[:, :3 * DN_DIM].astype(q_ref.dtype)
    zg_ref[:, :DN_DIM] = w_ref[:, 3 * DN_DIM:z_end].astype(zg_ref.dtype)
    gate = w_ref[:, z_end:z_end + LANES]
    lane = lax.broadcasted_iota(jnp.int32, gate.shape, 1)
    zg_ref[:, DN_DIM:] = jnp.where(lane < N_GATE_COLS, gate, 0.0).astype(zg_ref.dtype)
    gd = SC_GROUP_DIM
    for g in range(SC_GROUPS):
        for seg in range(3):
            src = g_end + seg * SC_DIM + g * gd
            sc_ref[:, (3 * g + seg) * gd:(3 * g + seg + 1) * gd] = w_ref[:, src:src + gd].astype(sc_ref.dtype)


def _prepare_w_in(w_in, tk=256):
    depth, d, cols = w_in.shape
    out = lambda width: jax.ShapeDtypeStruct((depth, d, width), BF16)
    spec = lambda width: pl.BlockSpec((None, tk, width), lambda l, i: (l, i, 0))
    return pl.pallas_call(
        _win_kernel,
        out_shape=(out(3 * DN_DIM), out(DN_DIM + LANES), out(3 * SC_DIM)),
        grid=(depth, d // tk),
        in_specs=[spec(cols)],
        out_specs=(spec(3 * DN_DIM), spec(DN_DIM + LANES), spec(3 * SC_DIM)),
        compiler_params=_params(("parallel", "parallel")),
        name="w_in_prep",
    )(w_in)


def _prepare(w_in, a_log, dt_bias):
    depth = w_in.shape[0]
    w_qkv, w_zg, w_sc = _prepare_w_in(w_in)
    vpad = jnp.zeros((depth, 1, LANES - 2 * DN_HEADS), F32)
    alog = jnp.concatenate([a_log.reshape(depth, 1, 2 * DN_HEADS).astype(F32), vpad], axis=-1)
    dtb = jnp.concatenate([dt_bias.reshape(depth, 1, 2 * DN_HEADS).astype(F32), vpad], axis=-1)
    return w_qkv, w_zg, w_sc, alog, dtb


def _trunk(x, prm):
    (norm_mix_pre, w_qkv, w_zg, w_sc, alog, dtb, conv_qkv, dn_norm, conv_sc, sc_norm, w_out,
     norm_mix_post, norm_ffn_pre, w_up, conv_ffn, w_down, norm_ffn_post) = prm
    depth = w_qkv.shape[0]
    n, t, _ = x.shape
    tm, tb = _tiles(t)
    h = _prenorm(x, norm_mix_pre[0], tm)
    for l in range(depth):
        qkv, zs, gates, ysc, k_t = _in_proj(h, w_qkv[l], w_zg[l], w_sc[l], conv_qkv[l], conv_sc[l], sc_norm[l],
                                            alog[l], dtb[l], tm)
        gates_t = gates[:, :, :N_GATE_COLS].reshape(n, t // CHUNK, CHUNK, 4, DN_HEADS // 2, 2)
        gates_t = gates_t.transpose(0, 1, 3, 4, 5, 2).reshape(n, t // CHUNK, N_GATE_COLS // 2, 2 * CHUNK)
        o_f, o_b = _delta_scan(qkv, gates, gates_t, k_t, tb)
        x, h = _out_proj(o_f, o_b, zs, ysc, x, w_out[l], dn_norm[l], norm_mix_post[l], norm_ffn_pre[l], tm)
        last = l == depth - 1
        x, *h = _ffn(h, x, w_up[l], conv_ffn[l], w_down[l], norm_ffn_post[l],
                     norm_mix_pre[0 if last else l + 1], not last, tm)
        h = h[0] if h else None
    return x


def kernel(x_prompt, x_sample, norm_mix_pre, w_in, conv_qkv, a_log, dt_bias, dn_norm, conv_sc, sc_norm, w_out, norm_mix_post, norm_ffn_pre, w_up, conv_ffn, w_down, norm_ffn_post):
    w_qkv, w_zg, w_sc, alog, dtb = _prepare(w_in, a_log, dt_bias)
    prm = (norm_mix_pre, w_qkv, w_zg, w_sc, alog, dtb, conv_qkv, dn_norm, conv_sc, sc_norm,
           w_out.astype(BF16), norm_mix_post, norm_ffn_pre, w_up.astype(BF16), conv_ffn,
           w_down.astype(BF16), norm_ffn_post)
    return (_trunk(x_prompt, prm), _trunk(x_sample, prm))
```

```python
import functools
import math

import jax
import jax.numpy as jnp
from jax import lax
from jax.experimental import pallas as pl
from jax.experimental.pallas import tpu as pltpu

D_MODEL = 2048
DN_HEADS = 8
DN_HEAD_DIM = 128
DN_DIM = DN_HEADS * DN_HEAD_DIM
SC_GROUPS = 8
SC_GROUP_DIM = 128
SC_DIM = SC_GROUPS * SC_GROUP_DIM
D_FF = 5632
CHUNK = 64
N_GATE_COLS = 4 * DN_HEADS
NORM_EPS = 1e-6
L2_EPS = 1e-6
Q_SCALE = DN_HEAD_DIM ** -0.5

LANES = 128
HALO = 16
VMEM_LIMIT = 56 * 1024 * 1024

F32 = jnp.float32
BF16 = jnp.bfloat16
NT_DIMS = (((1,), (1,)), ((), ()))
TN_DIMS = (((0,), (0,)), ((), ()))


def _sigmoid(x):
    return 1.0 / (1.0 + jnp.exp(-x))


def _silu(x):
    return x * _sigmoid(x)


def _softplus(x):
    return jnp.maximum(x, 0.0) + jnp.log1p(jnp.exp(-jnp.abs(x)))


def _rms(x):
    return x * lax.rsqrt(jnp.mean(x * x, axis=-1, keepdims=True) + NORM_EPS)


def _bdot(a, b):
    return jnp.dot(a, b, preferred_element_type=F32)


def _conv3_rows(p, cw, tm):
    rows = p.shape[0]
    prev = pltpu.roll(p, 1, 0)[HALO:HALO + tm]
    nxt = pltpu.roll(p, rows - 1, 0)[HALO:HALO + tm]
    mid = p[HALO:HALO + tm]
    return prev * cw[0:1] + mid * cw[1:2] + nxt * cw[2:3]


def _fill_halo(hx_ref, h_ref, hp_ref, hn_ref, tm, first, last):
    hp = hp_ref[...]
    hn = hn_ref[...]
    hx_ref[0:HALO, :] = jnp.where(first, jnp.zeros_like(hp), hp)
    hx_ref[HALO:HALO + tm, :] = h_ref[...]
    hx_ref[HALO + tm:HALO + tm + HALO, :] = jnp.where(last, jnp.zeros_like(hn), hn)


def _with_halo(h_ref, hp_ref, hn_ref, first, last):
    hp = hp_ref[...]
    hn = hn_ref[...]
    return jnp.concatenate([jnp.where(first, jnp.zeros_like(hp), hp), h_ref[...],
                            jnp.where(last, jnp.zeros_like(hn), hn)], axis=0)


def _halo_specs(tm, t_len, width):
    per = tm // HALO
    nblk = t_len // HALO
    main = pl.BlockSpec((None, tm, width), lambda n, i, *_: (n, i, 0))
    prev = pl.BlockSpec((None, HALO, width), lambda n, i, *_: (n, jnp.maximum(i * per - 1, 0), 0))
    nxt = pl.BlockSpec((None, HALO, width), lambda n, i, *_: (n, jnp.minimum((i + 1) * per, nblk - 1), 0))
    return main, prev, nxt


def _params(sem):
    return pltpu.CompilerParams(dimension_semantics=sem, vmem_limit_bytes=VMEM_LIMIT)


def _prenorm_kernel(x_ref, w_ref, o_ref):
    o_ref[...] = (_rms(x_ref[...]) * w_ref[...]).astype(o_ref.dtype)


def _prenorm(x, w, tm):
    n, t, d = x.shape
    return pl.pallas_call(
        _prenorm_kernel,
        out_shape=jax.ShapeDtypeStruct((n, t, d), BF16),
        grid=(n, t // tm),
        in_specs=[pl.BlockSpec((None, tm, d), lambda a, i: (a, i, 0)),
                  pl.BlockSpec((1, d), lambda a, i: (0, 0))],
        out_specs=pl.BlockSpec((None, tm, d), lambda a, i: (a, i, 0)),
        compiler_params=_params(("parallel", "parallel")),
        name="prenorm",
    )(x, w.reshape(1, d))


QKV_TILE = 512
SC_TILE_GROUPS = 2


def _pipelined(tasks):
    pending = None
    for matmul, epilogue in tasks:
        val = matmul()
        if pending is not None:
            pending[0](pending[1])
        pending = (epilogue, val)
    pending[0](pending[1])


def _split3(x):
    hi = x.astype(BF16)
    r1 = x - hi.astype(F32)
    mid = r1.astype(BF16)
    lo = (r1 - mid.astype(F32)).astype(BF16)
    return hi, mid, lo


def _inproj_kernel(h_ref, hp_ref, hn_ref, wq_ref, wz_ref, ws_ref, cwq_ref, cws_ref, gain_ref, alog_ref, dtb_ref,
                   qkv_ref, z_ref, g_ref, ysc_ref, kt_ref, gt_ref, *, tm):
    i = pl.program_id(1)
    hx = _with_halo(h_ref, hp_ref, hn_ref, i == 0, i == pl.num_programs(1) - 1)
    h = h_ref[...]
    gd = SC_GROUP_DIM
    sc_tile = SC_TILE_GROUPS * 3 * gd

    def halo_matmul(w_ref, c0, width):
        return _bdot(hx, w_ref[:, c0:c0 + width])

    def plain_matmul(w_ref, c0, width):
        return _bdot(h, w_ref[:, c0:c0 + width])

    def gate_epilogue(gl):
        g = -jnp.exp(alog_ref[...]) * _softplus(gl + dtb_ref[...])
        beta = _sigmoid(gl)
        hi, mid, lo = _split3(g)
        r = lax.broadcasted_iota(jnp.int32, (CHUNK, 3 * CHUNK), 0)
        c = lax.broadcasted_iota(jnp.int32, (CHUNK, 3 * CHUNK), 1) % CHUNK
        m_fwd = (c <= r).astype(BF16)
        m_bwd = (c >= r).astype(BF16)
        lane = lax.broadcasted_iota(jnp.int32, (CHUNK, LANES), 1)
        for ci in range(tm // CHUNK):
            rows = slice(ci * CHUNK, (ci + 1) * CHUNK)
            g3 = jnp.concatenate([hi[rows], mid[rows], lo[rows]], axis=0)
            cum_f = _bdot(m_fwd, g3)
            cum_b = _bdot(m_bwd, g3)
            tile = jnp.where(lane < DN_HEADS, cum_f, jnp.where(lane < 2 * DN_HEADS, cum_b, beta[rows]))
            g_ref[rows, :] = tile
            gt_ref[ci] = tile.T[:N_GATE_COLS, :]

    def qkv_epilogue(c0, p):
        y = _silu(_conv3_rows(p, cwq_ref[:, c0:c0 + QKV_TILE], tm))
        if c0 >= 2 * DN_DIM:
            qkv_ref[:, c0:c0 + QKV_TILE] = y.astype(qkv_ref.dtype)
            return
        heads = []
        for hh in range(QKV_TILE // DN_HEAD_DIM):
            yh = y[:, hh * DN_HEAD_DIM:(hh + 1) * DN_HEAD_DIM]
            ss = jnp.sum(yh * yh, axis=-1, keepdims=True)
            sl = slice(c0 + hh * DN_HEAD_DIM, c0 + (hh + 1) * DN_HEAD_DIM)
            heads.append(yh * lax.rsqrt(ss + L2_EPS))
            qkv_ref[:, sl] = heads[-1].astype(qkv_ref.dtype)
        if c0 >= DN_DIM:
            pair0 = (c0 - DN_DIM) // (2 * DN_HEAD_DIM)
            for ci in range(tm // CHUNK):
                rows = slice(ci * CHUNK, (ci + 1) * CHUNK)
                for pp in range(len(heads) // 2):
                    both = jnp.concatenate([heads[2 * pp][rows], heads[2 * pp + 1][rows]], axis=0)
                    kt_ref[ci, pair0 + pp] = both.T.astype(kt_ref.dtype)

    def z_epilogue(c0, p):
        z_ref[:, c0:c0 + p.shape[1]] = _silu(p).astype(z_ref.dtype)

    def sc_epilogue(jn, p):
        for gi in range(SC_TILE_GROUPS):
            base = gi * 3 * gd
            b = p[HALO:HALO + tm, base:base + gd]
            cx = p[:, base + gd:base + 2 * gd] * p[:, base + 2 * gd:base + 3 * gd]
            g0 = (jn * SC_TILE_GROUPS + gi) * gd
            y = b * _conv3_rows(cx, cws_ref[:, g0:g0 + gd], tm)
            ysc_ref[:, g0:g0 + gd] = (_rms(y) * gain_ref[:, g0:g0 + gd]).astype(ysc_ref.dtype)

    part = functools.partial
    zq = QKV_TILE // 2

    def z_gate_epilogue(p):
        z_epilogue(DN_DIM - zq, p[:, :zq])
        gate_epilogue(p[:, zq:])

    tasks = [(part(plain_matmul, wz_ref, DN_DIM - zq, zq + LANES), z_gate_epilogue)]
    tasks += [(part(halo_matmul, wq_ref, c0, QKV_TILE), part(qkv_epilogue, c0))
              for c0 in range(0, 3 * DN_DIM, QKV_TILE)]
    tasks += [(part(plain_matmul, wz_ref, DN_DIM - 2 * zq, zq), part(z_epilogue, DN_DIM - 2 * zq))]
    tasks += [(part(halo_matmul, ws_ref, jn * sc_tile, sc_tile), part(sc_epilogue, jn))
              for jn in range(SC_GROUPS // SC_TILE_GROUPS)]
    tasks += [(part(plain_matmul, wz_ref, c0, QKV_TILE), part(z_epilogue, c0))
              for c0 in range(0, DN_DIM - 2 * zq, QKV_TILE)]
    _pipelined(tasks)


def _layer_resident(w, layer):
    return pl.BlockSpec((None,) + w.shape[1:], lambda *_: (layer, 0, 0), pipeline_mode=pl.Buffered(1))


def _in_proj(h, w_qkv, w_zg, w_sc, layer, cw_qkv, cw_sc, gain, alog, dtb, tm):
    n, t, d = h.shape
    main, prev, nxt = _halo_specs(tm, t, d)
    whole = lambda shape: pl.BlockSpec(shape, lambda a, i: (0, 0))
    row = lambda width: pl.BlockSpec((None, tm, width), lambda a, i: (a, i, 0))
    return pl.pallas_call(
        functools.partial(_inproj_kernel, tm=tm),
        out_shape=(jax.ShapeDtypeStruct((n, t, 3 * DN_DIM), BF16),
                   jax.ShapeDtypeStruct((n, t, DN_DIM), BF16),
                   jax.ShapeDtypeStruct((n, t, LANES), F32),
                   jax.ShapeDtypeStruct((n, t, SC_DIM), BF16),
                   jax.ShapeDtypeStruct((n, t // CHUNK, DN_HEADS // 2, DN_HEAD_DIM, 2 * CHUNK), BF16),
                   jax.ShapeDtypeStruct((n, t // CHUNK, N_GATE_COLS, CHUNK), F32)),
        grid=(n, t // tm),
        in_specs=[main, prev, nxt,
                  _layer_resident(w_qkv, layer), _layer_resident(w_zg, layer), _layer_resident(w_sc, layer),
                  whole((3, 3 * DN_DIM)), whole((3, SC_DIM)), whole((1, SC_DIM)),
                  whole((1, LANES)), whole((1, LANES))],
        out_specs=(row(3 * DN_DIM), row(DN_DIM), row(LANES), row(SC_DIM),
                   pl.BlockSpec((None, tm // CHUNK, DN_HEADS // 2, DN_HEAD_DIM, 2 * CHUNK),
                                lambda a, i: (a, i, 0, 0, 0)),
                   pl.BlockSpec((None, tm // CHUNK, N_GATE_COLS, CHUNK), lambda a, i: (a, i, 0, 0))),
        compiler_params=_params(("parallel", "parallel")),
        name="in_proj",
    )(h, h, h, w_qkv, w_zg, w_sc, cw_qkv, cw_sc, gain.reshape(1, SC_DIM), alog, dtb)


DELTA_UNROLL = 8
GROUP = 2


def _delta_kernel(qf_ref, kf_ref, vf_ref, gcf_ref, grf_ref, ktf_ref, qb_ref, kb_ref, vb_ref, gcb_ref, grb_ref,
                  ktb_ref, of_ref, ob_ref, s_ref, *, tb):
    @pl.when(pl.program_id(1) == 0)
    def _():
        s_ref[...] = jnp.zeros_like(s_ref)

    nch = tb // CHUNK
    unroll = math.gcd(nch, DELTA_UNROLL)
    pair_w = 2 * DN_HEAD_DIM
    r = lax.broadcasted_iota(jnp.int32, (CHUNK, 2 * CHUNK), 0)
    lane = lax.broadcasted_iota(jnp.int32, (CHUNK, 2 * CHUNK), 1)
    c = lane % CHUNK
    second = lane >= CHUNK
    eye = (r == c).astype(F32)
    masks = ((r >= c, r > c), (r <= c, r < c))
    row_head = lambda width: lax.broadcasted_iota(jnp.int32, (2 * CHUNK, width), 0) // CHUNK
    col_head = lambda width, per: lax.broadcasted_iota(jnp.int32, (2 * CHUNK, width), 1) // per
    diag_sq = (row_head(2 * CHUNK) == col_head(2 * CHUNK, CHUNK)).astype(BF16)
    diag_wide = (row_head(pair_w) == col_head(pair_w, DN_HEAD_DIM)).astype(BF16)
    zero_s = jnp.zeros((DN_HEAD_DIM, DN_HEAD_DIM), BF16)
    refs = ((qf_ref, kf_ref, vf_ref, gcf_ref, grf_ref, ktf_ref, of_ref),
            (qb_ref, kb_ref, vb_ref, gcb_ref, grb_ref, ktb_ref, ob_ref))

    def ld(ch, name):
        return ch[name + "_ref"][ch["rows"], ch["sl"]]

    def block_diag(x, sel):
        return jnp.concatenate([x, x], axis=0) * sel

    def pair_cols(x, h0):
        shape = (CHUNK, 2 * CHUNK)
        return jnp.where(second, jnp.broadcast_to(x[:, h0 + 1:h0 + 2], shape), jnp.broadcast_to(x[:, h0:h0 + 1], shape))

    def chains_of(jj):
        chains = []
        for d, (q_ref, k_ref, v_ref, gc_ref, gr_ref, kt_ref, o_ref) in enumerate(refs):
            cc = (nch - 1 - jj) if d else jj
            rows = pl.ds(pl.multiple_of(cc * CHUNK, CHUNK), CHUNK)
            gcol = gc_ref[rows, :]
            grow = gr_ref[cc]
            goff = d * DN_HEADS
            boff = 2 * DN_HEADS + goff
            cum_c = gcol[:, goff:goff + DN_HEADS]
            b_c = gcol[:, boff:boff + DN_HEADS]
            last = 0 if d else CHUNK - 1
            gl_c = cum_c[last:last + 1, :]
            eg_c = jnp.exp(cum_c) * Q_SCALE
            ekd_c = jnp.exp(gl_c - cum_c)
            egl = jnp.exp(gl_c)
            npair = DN_HEADS // 2
            for p in range(npair):
                h0 = 2 * p
                gr2 = grow[d * npair + p:d * npair + p + 1, :]
                br2 = grow[(2 + d) * npair + p:(2 + d) * npair + p + 1, :]
                chains.append(dict(
                    d=d, h0=h0, rows=rows, sl=slice(p * pair_w, (p + 1) * pair_w),
                    q_ref=q_ref, k_ref=k_ref, v_ref=v_ref, o_ref=o_ref, mle=masks[d][0], strict=masks[d][1],
                    kt_ref=kt_ref, cc=cc, p=p,
                    gc=pair_cols(cum_c, h0), bc=pair_cols(b_c, h0), gr=gr2, br=br2, ebr=br2 * jnp.exp(gr2),
                    egc=[eg_c[:, h0 + j:h0 + j + 1] for j in range(2)],
                    ekd=[ekd_c[:, h0 + j:h0 + j + 1] for j in range(2)],
                    egl=[egl[:, h0 + j:h0 + j + 1] for j in range(2)]))
        return chains

    def head(x, j):
        return x[:, j * DN_HEAD_DIM:(j + 1) * DN_HEAD_DIM]

    def bd_sq(x):
        return block_diag(x.astype(BF16), diag_sq)

    def state_free(chains, out):
        decay = [jnp.where(ch["mle"], jnp.exp(jnp.where(ch["mle"], ch["gc"] - ch["gr"], 0.0)), 0.0)
                 for ch in chains]
        bdk = [block_diag(ld(ch, "k"), diag_wide) for ch in chains]
        kq = [lax.dot_general(jnp.concatenate([ld(ch, "k"), ld(ch, "q")], axis=0), x, NT_DIMS,
                              preferred_element_type=F32) for ch, x in zip(chains, bdk)]
        yield
        a_qk = [(x[CHUNK:] * dc * Q_SCALE).astype(BF16) for x, dc in zip(kq, decay)]
        pw = [jnp.where(ch["strict"], -(x[:CHUNK] * dc) * ch["bc"], 0.0) for ch, x, dc in zip(chains, kq, decay)]
        tinv = [eye + p for p in pw]
        pw = [_bdot(p.astype(BF16), bd_sq(p)) for p in pw]
        yield
        for _ in range(4):
            both = [_bdot(jnp.concatenate([t, p], axis=0).astype(BF16), bd_sq(p)) for t, p in zip(tinv, pw)]
            tinv = [t + b[:CHUNK] for t, b in zip(tinv, both)]
            pw = [b[CHUNK:] for b in both]
            yield
        tinv = [t + _bdot(t.astype(BF16), bd_sq(p)) for t, p in zip(tinv, pw)]
        yield
        bdv = [block_diag(ld(ch, "v"), diag_wide) for ch in chains]
        u = [_bdot((t * ch["br"]).astype(BF16), x) for ch, t, x in zip(chains, tinv, bdv)]
        w = [_bdot((t * ch["ebr"]).astype(BF16), x) for ch, t, x in zip(chains, tinv, bdk)]
        qg = [jnp.concatenate([(head(ld(ch, "q"), j).astype(F32) * ch["egc"][j]).astype(BF16) for j in range(2)],
                              axis=1) for ch in chains]
        out.extend(zip(a_qk, u, w, qg))

    def state_step(chains, free, s):
        s_bd = [jnp.concatenate([jnp.concatenate([z[0].astype(BF16), zero_s], axis=1),
                                 jnp.concatenate([zero_s, z[1].astype(BF16)], axis=1)], axis=0) for z in s]
        wq = [_bdot(jnp.concatenate([w.astype(BF16), qg], axis=0), z)
              for (_, _, w, qg), z in zip(free, s_bd)]
        yield
        vn = [u - y[:CHUNK] for (_, u, _, _), y in zip(free, wq)]
        for ch, y, (a_qk, _, _, _), x in zip(chains, wq, free, vn):
            ch["o_ref"][ch["rows"], ch["sl"]] = y[CHUNK:] + _bdot(a_qk, block_diag(x.astype(BF16), diag_wide))
        vd = [jnp.concatenate([(head(x, j) * ch["ekd"][j]).astype(BF16) for j in range(2)], axis=1)
              for ch, x in zip(chains, vn)]
        upd = [_bdot(ch["kt_ref"][ch["cc"], ch["p"]], block_diag(x, diag_wide)) for ch, x in zip(chains, vd)]
        yield
        s[:] = [[z[j] * ch["egl"][j] + head(y, j) for j in range(2)] for ch, z, y in zip(chains, s, upd)]

    def alternate(first, second):
        for _ in first:
            next(second, None)
        for _ in second:
            pass

    def body(it, carry):
        chunks = [chains_of(it * unroll + j) for j in range(unroll)]
        s = [[s_ref[ch["d"], ch["h0"] + j] for j in range(2)] for ch in chunks[0]]
        free = []

        def free_of(lo, hi):
            return state_free([ch for chains in chunks[lo:hi] for ch in chains], free)

        def steps(lo, hi):
            for j in range(lo, hi):
                yield from state_step(chunks[j], free[j * len(chunks[j]):(j + 1) * len(chunks[j])], s)

        bounds = list(range(0, unroll, GROUP)) + [unroll]
        alternate(free_of(bounds[0], bounds[1]), iter(()))
        for g in range(1, len(bounds) - 1):
            alternate(free_of(bounds[g], bounds[g + 1]), steps(bounds[g - 1], bounds[g]))
        alternate(steps(bounds[-2], bounds[-1]), iter(()))
        for ch, z in zip(chunks[0], s):
            for j in range(2):
                s_ref[ch["d"], ch["h0"] + j] = z[j]
        return carry

    lax.fori_loop(0, nch // unroll, body, 0)


def _delta_scan(qkv, gates, gates_t, k_t, tb):
    n, t, _ = qkv.shape
    nb = t // tb
    nch = tb // CHUNK
    fwd = lambda i: i
    bwd = lambda i: nb - 1 - i

    def specs(blk):
        return [pl.BlockSpec((None, tb, DN_DIM), lambda a, i: (a, blk(i), 0)),
                pl.BlockSpec((None, tb, DN_DIM), lambda a, i: (a, blk(i), 1)),
                pl.BlockSpec((None, tb, DN_DIM), lambda a, i: (a, blk(i), 2)),
                pl.BlockSpec((None, tb, LANES), lambda a, i: (a, blk(i), 0)),
                pl.BlockSpec((None, nch, N_GATE_COLS // 2, 2 * CHUNK), lambda a, i: (a, blk(i), 0, 0)),
                pl.BlockSpec((None, nch, DN_HEADS // 2, DN_HEAD_DIM, 2 * CHUNK), lambda a, i: (a, blk(i), 0, 0, 0))]

    out = jax.ShapeDtypeStruct((n, t, DN_DIM), F32)
    args = (qkv, qkv, qkv, gates, gates_t, k_t)
    return pl.pallas_call(
        functools.partial(_delta_kernel, tb=tb),
        out_shape=(out, out),
        grid=(n, nb),
        in_specs=specs(fwd) + specs(bwd),
        out_specs=(pl.BlockSpec((None, tb, DN_DIM), lambda a, i: (a, fwd(i), 0)),
                   pl.BlockSpec((None, tb, DN_DIM), lambda a, i: (a, bwd(i), 0))),
        scratch_shapes=[pltpu.VMEM((2, DN_HEADS, DN_HEAD_DIM, DN_HEAD_DIM), F32)],
        compiler_params=_params(("parallel", "arbitrary")),
        name="delta_scan",
    )(*args, *args)


OUT_ROWS = 256


def _out_kernel(of_ref, ob_ref, zs_ref, ysc_ref, x_ref, w_ref, dnw_ref, npost_ref, nnext_ref,
                xo_ref, hn_ref, odn_ref, *, tm):
    def matmul(r0):
        rows = slice(r0, r0 + OUT_ROWS)
        for hh in range(DN_HEADS):
            sl = slice(hh * DN_HEAD_DIM, (hh + 1) * DN_HEAD_DIM)
            o = of_ref[rows, sl] + ob_ref[rows, sl]
            y = _rms(o) * dnw_ref[...] * zs_ref[rows, sl].astype(F32)
            odn_ref[rows, sl] = y.astype(odn_ref.dtype)
        return _bdot(odn_ref[rows, :], w_ref[0:DN_DIM, :]) + _bdot(ysc_ref[rows, :], w_ref[DN_DIM:, :])

    def epilogue(r0, m):
        rows = slice(r0, r0 + OUT_ROWS)
        xn = x_ref[rows, :] + _rms(m) * npost_ref[...]
        xo_ref[rows, :] = xn
        hn_ref[rows, :] = (_rms(xn) * nnext_ref[...]).astype(hn_ref.dtype)

    _pipelined([(functools.partial(matmul, r0), functools.partial(epilogue, r0))
                for r0 in range(0, tm, OUT_ROWS)])


def _out_proj(o_f, o_b, zs, ysc, x, w, layer, dn_w, n_post, n_next, tm):
    n, t, d = x.shape
    assert tm % OUT_ROWS == 0
    row = lambda width: pl.BlockSpec((None, tm, width), lambda a, i: (a, i, 0))
    vec = lambda width: pl.BlockSpec((1, width), lambda a, i: (0, 0))
    return pl.pallas_call(
        functools.partial(_out_kernel, tm=tm),
        out_shape=(jax.ShapeDtypeStruct((n, t, d), F32), jax.ShapeDtypeStruct((n, t, d), BF16)),
        grid=(n, t // tm),
        in_specs=[row(DN_DIM), row(DN_DIM), row(DN_DIM), row(SC_DIM), row(d),
                  _layer_resident(w, layer),
                  vec(DN_HEAD_DIM), vec(d), vec(d)],
        out_specs=(row(d), row(d)),
        scratch_shapes=[pltpu.VMEM((tm, DN_DIM), BF16)],
        compiler_params=_params(("parallel", "parallel")),
        name="out_proj",
    )(o_f, o_b, zs, ysc, x, w, dn_w.reshape(1, -1), n_post.reshape(1, d), n_next.reshape(1, d))


def _ffn_kernel(h_ref, hp_ref, hn_ref, wa_ref, wb_ref, cw_ref, wd_ref, x_ref, npost_ref, nnext_ref,
                xo_ref, *rest, tm, emit_next):
    ho_ref = rest[0] if emit_next else None
    hx_ref, acc_ref = rest[-2:]
    i = pl.program_id(1)
    f = pl.program_id(2)

    @pl.when(f == 0)
    def _():
        _fill_halo(hx_ref, h_ref, hp_ref, hn_ref, tm, i == 0, i == pl.num_programs(1) - 1)
        acc_ref[...] = jnp.zeros_like(acc_ref)

    a = _bdot(hx_ref[...], wa_ref[...])
    b = _bdot(h_ref[...], wb_ref[...])
    act = (_silu(_conv3_rows(a, cw_ref[...], tm)) * b).astype(BF16)
    acc_ref[...] += _bdot(act, wd_ref[...])

    @pl.when(f == pl.num_programs(2) - 1)
    def _():
        xn = x_ref[...] + _rms(acc_ref[...]) * npost_ref[...]
        xo_ref[...] = xn
        if emit_next:
            ho_ref[...] = (_rms(xn) * nnext_ref[...]).astype(ho_ref.dtype)


def _ffn(h, x, w_up, cw, w_down, layer, n_post, n_next, emit_next, tm, tf=512):
    n, t, d = x.shape
    nf = D_FF // tf
    main, prev, nxt = _halo_specs(tm, t, d)
    row = lambda: pl.BlockSpec((None, tm, d), lambda a, i, f: (a, i, 0))
    vec = lambda: pl.BlockSpec((1, d), lambda a, i, f: (0, 0))
    outs = (jax.ShapeDtypeStruct((n, t, d), F32), jax.ShapeDtypeStruct((n, t, d), BF16))[:2 if emit_next else 1]
    return pl.pallas_call(
        functools.partial(_ffn_kernel, tm=tm, emit_next=emit_next),
        out_shape=outs,
        grid=(n, t // tm, nf),
        in_specs=[main, prev, nxt,
                  pl.BlockSpec((None, d, tf), lambda a, i, f: (layer, 0, f)),
                  pl.BlockSpec((None, d, tf), lambda a, i, f: (layer, 0, f + nf)),
                  pl.BlockSpec((3, tf), lambda a, i, f: (0, f)),
                  pl.BlockSpec((None, tf, d), lambda a, i, f: (layer, f, 0)),
                  row(), vec(), vec()],
        out_specs=tuple(row() for _ in outs),
        scratch_shapes=[pltpu.VMEM((tm + 2 * HALO, d), BF16), pltpu.VMEM((tm, d), F32)],
        compiler_params=_params(("parallel", "parallel", "arbitrary")),
        name="ffn",
    )(h, h, h, w_up, w_up, cw, w_down, x, n_post.reshape(1, d), n_next.reshape(1, d))


def _tiles(t):
    return min(512, t), min(DELTA_UNROLL * CHUNK, t)


def _win_kernel(w_ref, q_ref, zg_ref, sc_ref):
    z_end = 4 * DN_DIM
    g_end = z_end + N_GATE_COLS
    q_ref[...] = w_ref[:, :3 * DN_DIM].astype(q_ref.dtype)
    zg_ref[:, :DN_DIM] = w_ref[:, 3 * DN_DIM:z_end].astype(zg_ref.dtype)
    gate = w_ref[:, z_end:z_end + LANES]
    lane = lax.broadcasted_iota(jnp.int32, gate.shape, 1)
    zg_ref[:, DN_DIM:] = jnp.where(lane < N_GATE_COLS, gate, 0.0).astype(zg_ref.dtype)
    gd = SC_GROUP_DIM
    for g in range(SC_GROUPS):
        for seg in range(3):
            src = g_end + seg * SC_DIM + g * gd
            sc_ref[:, (3 * g + seg) * gd:(3 * g + seg + 1) * gd] = w_ref[:, src:src + gd].astype(sc_ref.dtype)


def _prepare_w_in(w_in, tk=256):
    depth, d, cols = w_in.shape
    out = lambda width: jax.ShapeDtypeStruct((depth, d, width), BF16)
    spec = lambda width: pl.BlockSpec((None, tk, width), lambda l, i: (l, i, 0))
    return pl.pallas_call(
        _win_kernel,
        out_shape=(out(3 * DN_DIM), out(DN_DIM + LANES), out(3 * SC_DIM)),
        grid=(depth, d // tk),
        in_specs=[spec(cols)],
        out_specs=(spec(3 * DN_DIM), spec(DN_DIM + LANES), spec(3 * SC_DIM)),
        compiler_params=_params(("parallel", "parallel")),
        name="w_in_prep",
    )(w_in)


def _prepare(w_in, a_log, dt_bias):
    depth = w_in.shape[0]
    w_qkv, w_zg, w_sc = _prepare_w_in(w_in)
    vpad = jnp.zeros((depth, 1, LANES - 2 * DN_HEADS), F32)
    alog = jnp.concatenate([a_log.reshape(depth, 1, 2 * DN_HEADS).astype(F32), vpad], axis=-1)
    dtb = jnp.concatenate([dt_bias.reshape(depth, 1, 2 * DN_HEADS).astype(F32), vpad], axis=-1)
    return w_qkv, w_zg, w_sc, alog, dtb


def _trunk(x, prm):
    (norm_mix_pre, w_qkv, w_zg, w_sc, alog, dtb, conv_qkv, dn_norm, conv_sc, sc_norm, w_out,
     norm_mix_post, norm_ffn_pre, w_up, conv_ffn, w_down, norm_ffn_post) = prm
    depth = w_qkv.shape[0]
    n, t, _ = x.shape
    tm, tb = _tiles(t)
    h = _prenorm(x, norm_mix_pre[0], tm)
    for l in range(depth):
        qkv, zs, gates, ysc, k_t, gates_t = _in_proj(h, w_qkv, w_zg, w_sc, l, conv_qkv[l], conv_sc[l], sc_norm[l],
                                                     alog[l], dtb[l], tm)
        gates_t = gates_t.reshape(n, t // CHUNK, N_GATE_COLS // 2, 2 * CHUNK)
        o_f, o_b = _delta_scan(qkv, gates, gates_t, k_t, tb)
        x, h = _out_proj(o_f, o_b, zs, ysc, x, w_out, l, dn_norm[l], norm_mix_post[l], norm_ffn_pre[l], tm)
        last = l == depth - 1
        x, *h = _ffn(h, x, w_up, conv_ffn[l], w_down, l, norm_ffn_post[l],
                     norm_mix_pre[0 if last else l + 1], not last, tm)
        h = h[0] if h else None
    return x


def kernel(x_prompt, x_sample, norm_mix_pre, w_in, conv_qkv, a_log, dt_bias, dn_norm, conv_sc, sc_norm, w_out, norm_mix_post, norm_ffn_pre, w_up, conv_ffn, w_down, norm_ffn_post):
    w_qkv, w_zg, w_sc, alog, dtb = _prepare(w_in, a_log, dt_bias)
    prm = (norm_mix_pre, w_qkv, w_zg, w_sc, alog, dtb, conv_qkv, dn_norm, conv_sc, sc_norm,
           w_out.astype(BF16), norm_mix_post, norm_ffn_pre, w_up.astype(BF16), conv_ffn,
           w_down.astype(BF16), norm_ffn_post)
    return (_trunk(x_prompt, prm), _trunk(x_sample, prm))
```

```python
import functools
import math

import jax
import jax.numpy as jnp
from jax import lax
from jax.experimental import pallas as pl
from jax.experimental.pallas import tpu as pltpu

D_MODEL = 2048
DN_HEADS = 8
DN_HEAD_DIM = 128
DN_DIM = DN_HEADS * DN_HEAD_DIM
SC_GROUPS = 8
SC_GROUP_DIM = 128
SC_DIM = SC_GROUPS * SC_GROUP_DIM
D_FF = 5632
CHUNK = 64
N_GATE_COLS = 4 * DN_HEADS
NORM_EPS = 1e-6
L2_EPS = 1e-6
Q_SCALE = DN_HEAD_DIM ** -0.5

LANES = 128
HALO = 16
VMEM_LIMIT = 56 * 1024 * 1024

F32 = jnp.float32
BF16 = jnp.bfloat16
NT_DIMS = (((1,), (1,)), ((), ()))
TN_DIMS = (((0,), (0,)), ((), ()))


def _sigmoid(x):
    return 1.0 / (1.0 + jnp.exp(-x))


def _silu(x):
    return x * _sigmoid(x)


def _softplus(x):
    return jnp.maximum(x, 0.0) + jnp.log1p(jnp.exp(-jnp.abs(x)))


def _rms(x):
    return x * lax.rsqrt(jnp.mean(x * x, axis=-1, keepdims=True) + NORM_EPS)


def _bdot(a, b):
    return jnp.dot(a, b, preferred_element_type=F32)


def _conv3_rows(p, cw, tm):
    rows = p.shape[0]
    prev = pltpu.roll(p, 1, 0)[HALO:HALO + tm]
    nxt = pltpu.roll(p, rows - 1, 0)[HALO:HALO + tm]
    mid = p[HALO:HALO + tm]
    return prev * cw[0:1] + mid * cw[1:2] + nxt * cw[2:3]


def _fill_halo(hx_ref, h_ref, hp_ref, hn_ref, tm, first, last):
    hp = hp_ref[...]
    hn = hn_ref[...]
    hx_ref[0:HALO, :] = jnp.where(first, jnp.zeros_like(hp), hp)
    hx_ref[HALO:HALO + tm, :] = h_ref[...]
    hx_ref[HALO + tm:HALO + tm + HALO, :] = jnp.where(last, jnp.zeros_like(hn), hn)


def _with_halo(h, hp, hn, first, last):
    return jnp.concatenate([jnp.where(first, jnp.zeros_like(hp), hp), h,
                            jnp.where(last, jnp.zeros_like(hn), hn)], axis=0)


def _halo_specs(tm, t_len, width):
    per = tm // HALO
    nblk = t_len // HALO
    main = pl.BlockSpec((None, tm, width), lambda n, i, *_: (n, i, 0))
    prev = pl.BlockSpec((None, HALO, width), lambda n, i, *_: (n, jnp.maximum(i * per - 1, 0), 0))
    nxt = pl.BlockSpec((None, HALO, width), lambda n, i, *_: (n, jnp.minimum((i + 1) * per, nblk - 1), 0))
    return main, prev, nxt


def _params(sem):
    return pltpu.CompilerParams(dimension_semantics=sem, vmem_limit_bytes=VMEM_LIMIT)


QKV_TILE = 512
SC_TILE_GROUPS = 2


def _pipelined(tasks):
    pending = None
    for matmul, epilogue in tasks:
        val = matmul()
        if pending is not None:
            pending[0](pending[1])
        pending = (epilogue, val)
    pending[0](pending[1])


def _split3(x):
    hi = x.astype(BF16)
    r1 = x - hi.astype(F32)
    mid = r1.astype(BF16)
    lo = (r1 - mid.astype(F32)).astype(BF16)
    return hi, mid, lo


def _inproj_kernel(h_ref, hp_ref, hn_ref, nw_ref, wq_ref, wz_ref, ws_ref, cwq_ref, cws_ref, gain_ref, alog_ref,
                   dtb_ref, qkv_ref, z_ref, g_ref, ysc_ref, kt_ref, gt_ref, *, tm, pre_norm):
    i = pl.program_id(1)

    def rows_of(ref):
        return (_rms(ref[...]) * nw_ref[...]).astype(BF16) if pre_norm else ref[...]

    h = rows_of(h_ref)
    hx = _with_halo(h, rows_of(hp_ref), rows_of(hn_ref), i == 0, i == pl.num_programs(1) - 1)
    gd = SC_GROUP_DIM
    sc_tile = SC_TILE_GROUPS * 3 * gd

    def halo_matmul(w_ref, c0, width):
        return _bdot(hx, w_ref[:, c0:c0 + width])

    def plain_matmul(w_ref, c0, width):
        return _bdot(h, w_ref[:, c0:c0 + width])

    def gate_epilogue(gl):
        g = -jnp.exp(alog_ref[...]) * _softplus(gl + dtb_ref[...])
        beta = _sigmoid(gl)
        hi, mid, lo = _split3(g)
        r = lax.broadcasted_iota(jnp.int32, (CHUNK, 3 * CHUNK), 0)
        c = lax.broadcasted_iota(jnp.int32, (CHUNK, 3 * CHUNK), 1) % CHUNK
        m_fwd = (c <= r).astype(BF16)
        m_bwd = (c >= r).astype(BF16)
        lane = lax.broadcasted_iota(jnp.int32, (CHUNK, LANES), 1)
        for ci in range(tm // CHUNK):
            rows = slice(ci * CHUNK, (ci + 1) * CHUNK)
            g3 = jnp.concatenate([hi[rows], mid[rows], lo[rows]], axis=0)
            cum_f = _bdot(m_fwd, g3)
            cum_b = _bdot(m_bwd, g3)
            tile = jnp.where(lane < DN_HEADS, cum_f, jnp.where(lane < 2 * DN_HEADS, cum_b, beta[rows]))
            g_ref[rows, :] = tile
            gt_ref[ci] = tile.T[:N_GATE_COLS, :]

    def qkv_epilogue(c0, p):
        y = _silu(_conv3_rows(p, cwq_ref[:, c0:c0 + QKV_TILE], tm))
        if c0 >= 2 * DN_DIM:
            qkv_ref[:, c0:c0 + QKV_TILE] = y.astype(qkv_ref.dtype)
            return
        heads = []
        for hh in range(QKV_TILE // DN_HEAD_DIM):
            yh = y[:, hh * DN_HEAD_DIM:(hh + 1) * DN_HEAD_DIM]
            ss = jnp.sum(yh * yh, axis=-1, keepdims=True)
            sl = slice(c0 + hh * DN_HEAD_DIM, c0 + (hh + 1) * DN_HEAD_DIM)
            heads.append(yh * lax.rsqrt(ss + L2_EPS))
            qkv_ref[:, sl] = heads[-1].astype(qkv_ref.dtype)
        if c0 >= DN_DIM:
            pair0 = (c0 - DN_DIM) // (2 * DN_HEAD_DIM)
            for ci in range(tm // CHUNK):
                rows = slice(ci * CHUNK, (ci + 1) * CHUNK)
                for pp in range(len(heads) // 2):
                    both = jnp.concatenate([heads[2 * pp][rows], heads[2 * pp + 1][rows]], axis=0)
                    kt_ref[ci, pair0 + pp] = both.T.astype(kt_ref.dtype)

    def z_epilogue(c0, p):
        z_ref[:, c0:c0 + p.shape[1]] = _silu(p).astype(z_ref.dtype)

    def sc_epilogue(jn, p):
        for gi in range(SC_TILE_GROUPS):
            base = gi * 3 * gd
            b = p[HALO:HALO + tm, base:base + gd]
            cx = p[:, base + gd:base + 2 * gd] * p[:, base + 2 * gd:base + 3 * gd]
            g0 = (jn * SC_TILE_GROUPS + gi) * gd
            y = b * _conv3_rows(cx, cws_ref[:, g0:g0 + gd], tm)
            ysc_ref[:, g0:g0 + gd] = (_rms(y) * gain_ref[:, g0:g0 + gd]).astype(ysc_ref.dtype)

    part = functools.partial
    zq = QKV_TILE // 2

    def z_gate_epilogue(p):
        z_epilogue(DN_DIM - zq, p[:, :zq])
        gate_epilogue(p[:, zq:])

    tasks = [(part(plain_matmul, wz_ref, DN_DIM - zq, zq + LANES), z_gate_epilogue)]
    tasks += [(part(halo_matmul, wq_ref, c0, QKV_TILE), part(qkv_epilogue, c0))
              for c0 in range(0, 3 * DN_DIM, QKV_TILE)]
    tasks += [(part(plain_matmul, wz_ref, DN_DIM - 2 * zq, zq), part(z_epilogue, DN_DIM - 2 * zq))]
    tasks += [(part(halo_matmul, ws_ref, jn * sc_tile, sc_tile), part(sc_epilogue, jn))
              for jn in range(SC_GROUPS // SC_TILE_GROUPS)]
    tasks += [(part(plain_matmul, wz_ref, c0, QKV_TILE), part(z_epilogue, c0))
              for c0 in range(0, DN_DIM - 2 * zq, QKV_TILE)]
    _pipelined(tasks)


def _layer_resident(w, layer):
    return pl.BlockSpec((None,) + w.shape[1:], lambda *_: (layer, 0, 0), pipeline_mode=pl.Buffered(1))


def _in_proj(h, n_pre, pre_norm, w_qkv, w_zg, w_sc, layer, cw_qkv, cw_sc, gain, alog, dtb, tm):
    n, t, d = h.shape
    main, prev, nxt = _halo_specs(tm, t, d)
    whole = lambda shape: pl.BlockSpec(shape, lambda a, i: (0, 0))
    row = lambda width: pl.BlockSpec((None, tm, width), lambda a, i: (a, i, 0))
    return pl.pallas_call(
        functools.partial(_inproj_kernel, tm=tm, pre_norm=pre_norm),
        out_shape=(jax.ShapeDtypeStruct((n, t, 3 * DN_DIM), BF16),
                   jax.ShapeDtypeStruct((n, t, DN_DIM), BF16),
                   jax.ShapeDtypeStruct((n, t, LANES), F32),
                   jax.ShapeDtypeStruct((n, t, SC_DIM), BF16),
                   jax.ShapeDtypeStruct((n, t // CHUNK, DN_HEADS // 2, DN_HEAD_DIM, 2 * CHUNK), BF16),
                   jax.ShapeDtypeStruct((n, t // CHUNK, N_GATE_COLS, CHUNK), F32)),
        grid=(n, t // tm),
        in_specs=[main, prev, nxt, whole((1, d)),
                  _layer_resident(w_qkv, layer), _layer_resident(w_zg, layer), _layer_resident(w_sc, layer),
                  whole((3, 3 * DN_DIM)), whole((3, SC_DIM)), whole((1, SC_DIM)),
                  whole((1, LANES)), whole((1, LANES))],
        out_specs=(row(3 * DN_DIM), row(DN_DIM), row(LANES), row(SC_DIM),
                   pl.BlockSpec((None, tm // CHUNK, DN_HEADS // 2, DN_HEAD_DIM, 2 * CHUNK),
                                lambda a, i: (a, i, 0, 0, 0)),
                   pl.BlockSpec((None, tm // CHUNK, N_GATE_COLS, CHUNK), lambda a, i: (a, i, 0, 0))),
        compiler_params=_params(("parallel", "parallel")),
        name="in_proj",
    )(h, h, h, n_pre.reshape(1, d), w_qkv, w_zg, w_sc, cw_qkv, cw_sc, gain.reshape(1, SC_DIM), alog, dtb)


DELTA_UNROLL = 8
GROUP = 2


def _delta_kernel(qf_ref, kf_ref, vf_ref, gcf_ref, grf_ref, ktf_ref, qb_ref, kb_ref, vb_ref, gcb_ref, grb_ref,
                  ktb_ref, of_ref, ob_ref, s_ref, *, tb):
    @pl.when(pl.program_id(1) == 0)
    def _():
        s_ref[...] = jnp.zeros_like(s_ref)

    nch = tb // CHUNK
    unroll = math.gcd(nch, DELTA_UNROLL)
    pair_w = 2 * DN_HEAD_DIM
    r = lax.broadcasted_iota(jnp.int32, (CHUNK, 2 * CHUNK), 0)
    lane = lax.broadcasted_iota(jnp.int32, (CHUNK, 2 * CHUNK), 1)
    c = lane % CHUNK
    second = lane >= CHUNK
    eye = (r == c).astype(F32)
    masks = ((r >= c, r > c), (r <= c, r < c))
    row_head = lambda width: lax.broadcasted_iota(jnp.int32, (2 * CHUNK, width), 0) // CHUNK
    col_head = lambda width, per: lax.broadcasted_iota(jnp.int32, (2 * CHUNK, width), 1) // per
    diag_sq = (row_head(2 * CHUNK) == col_head(2 * CHUNK, CHUNK)).astype(BF16)
    diag_wide = (row_head(pair_w) == col_head(pair_w, DN_HEAD_DIM)).astype(BF16)
    zero_s = jnp.zeros((DN_HEAD_DIM, DN_HEAD_DIM), BF16)
    refs = ((qf_ref, kf_ref, vf_ref, gcf_ref, grf_ref, ktf_ref, of_ref),
            (qb_ref, kb_ref, vb_ref, gcb_ref, grb_ref, ktb_ref, ob_ref))

    def ld(ch, name):
        return ch[name + "_ref"][ch["rows"], ch["sl"]]

    def block_diag(x, sel):
        return jnp.concatenate([x, x], axis=0) * sel

    def pair_cols(x, h0):
        shape = (CHUNK, 2 * CHUNK)
        return jnp.where(second, jnp.broadcast_to(x[:, h0 + 1:h0 + 2], shape), jnp.broadcast_to(x[:, h0:h0 + 1], shape))

    def chains_of(jj):
        chains = []
        for d, (q_ref, k_ref, v_ref, gc_ref, gr_ref, kt_ref, o_ref) in enumerate(refs):
            cc = (nch - 1 - jj) if d else jj
            rows = pl.ds(pl.multiple_of(cc * CHUNK, CHUNK), CHUNK)
            gcol = gc_ref[rows, :]
            grow = gr_ref[cc]
            goff = d * DN_HEADS
            boff = 2 * DN_HEADS + goff
            cum_c = gcol[:, goff:goff + DN_HEADS]
            b_c = gcol[:, boff:boff + DN_HEADS]
            last = 0 if d else CHUNK - 1
            gl_c = cum_c[last:last + 1, :]
            eg_c = jnp.exp(cum_c) * Q_SCALE
            ekd_c = jnp.exp(gl_c - cum_c)
            egl = jnp.exp(gl_c)
            npair = DN_HEADS // 2
            for p in range(npair):
                h0 = 2 * p
                gr2 = grow[d * npair + p:d * npair + p + 1, :]
                br2 = grow[(2 + d) * npair + p:(2 + d) * npair + p + 1, :]
                chains.append(dict(
                    d=d, h0=h0, rows=rows, sl=slice(p * pair_w, (p + 1) * pair_w),
                    q_ref=q_ref, k_ref=k_ref, v_ref=v_ref, o_ref=o_ref, mle=masks[d][0], strict=masks[d][1],
                    kt_ref=kt_ref, cc=cc, p=p,
                    gc=pair_cols(cum_c, h0), bc=pair_cols(b_c, h0), gr=gr2, br=br2, ebr=br2 * jnp.exp(gr2),
                    egc=[eg_c[:, h0 + j:h0 + j + 1] for j in range(2)],
                    ekd=[ekd_c[:, h0 + j:h0 + j + 1] for j in range(2)],
                    egl=[egl[:, h0 + j:h0 + j + 1] for j in range(2)]))
        return chains

    def head(x, j):
        return x[:, j * DN_HEAD_DIM:(j + 1) * DN_HEAD_DIM]

    def bd_sq(x):
        return block_diag(x.astype(BF16), diag_sq)

    def state_free(chains, out):
        decay = [jnp.where(ch["mle"], jnp.exp(jnp.where(ch["mle"], ch["gc"] - ch["gr"], 0.0)), 0.0)
                 for ch in chains]
        bdk = [block_diag(ld(ch, "k"), diag_wide) for ch in chains]
        kq = [lax.dot_general(jnp.concatenate([ld(ch, "k"), ld(ch, "q")], axis=0), x, NT_DIMS,
                              preferred_element_type=F32) for ch, x in zip(chains, bdk)]
        yield
        a_qk = [(x[CHUNK:] * dc * Q_SCALE).astype(BF16) for x, dc in zip(kq, decay)]
        pw = [jnp.where(ch["strict"], -(x[:CHUNK] * dc) * ch["bc"], 0.0) for ch, x, dc in zip(chains, kq, decay)]
        tinv = [eye + p for p in pw]
        pw = [_bdot(p.astype(BF16), bd_sq(p)) for p in pw]
        yield
        for _ in range(4):
            both = [_bdot(jnp.concatenate([t, p], axis=0).astype(BF16), bd_sq(p)) for t, p in zip(tinv, pw)]
            tinv = [t + b[:CHUNK] for t, b in zip(tinv, both)]
            pw = [b[CHUNK:] for b in both]
            yield
        tinv = [t + _bdot(t.astype(BF16), bd_sq(p)) for t, p in zip(tinv, pw)]
        yield
        bdv = [block_diag(ld(ch, "v"), diag_wide) for ch in chains]
        u = [_bdot((t * ch["br"]).astype(BF16), x) for ch, t, x in zip(chains, tinv, bdv)]
        w = [_bdot((t * ch["ebr"]).astype(BF16), x) for ch, t, x in zip(chains, tinv, bdk)]
        qg = [jnp.concatenate([(head(ld(ch, "q"), j).astype(F32) * ch["egc"][j]).astype(BF16) for j in range(2)],
                              axis=1) for ch in chains]
        out.extend(zip(a_qk, u, w, qg))

    def state_step(chains, free, s):
        s_bd = [jnp.concatenate([jnp.concatenate([z[0].astype(BF16), zero_s], axis=1),
                                 jnp.concatenate([zero_s, z[1].astype(BF16)], axis=1)], axis=0) for z in s]
        wq = [_bdot(jnp.concatenate([w.astype(BF16), qg], axis=0), z)
              for (_, _, w, qg), z in zip(free, s_bd)]
        yield
        vn = [u - y[:CHUNK] for (_, u, _, _), y in zip(free, wq)]
        for ch, y, (a_qk, _, _, _), x in zip(chains, wq, free, vn):
            o = y[CHUNK:] + _bdot(a_qk, block_diag(x.astype(BF16), diag_wide))
            ch["o_ref"][ch["rows"], ch["sl"]] = o.astype(ch["o_ref"].dtype)
        vd = [jnp.concatenate([(head(x, j) * ch["ekd"][j]).astype(BF16) for j in range(2)], axis=1)
              for ch, x in zip(chains, vn)]
        upd = [_bdot(ch["kt_ref"][ch["cc"], ch["p"]], block_diag(x, diag_wide)) for ch, x in zip(chains, vd)]
        yield
        s[:] = [[z[j] * ch["egl"][j] + head(y, j) for j in range(2)] for ch, z, y in zip(chains, s, upd)]

    def alternate(first, second):
        for _ in first:
            next(second, None)
        for _ in second:
            pass

    def body(it, carry):
        chunks = [chains_of(it * unroll + j) for j in range(unroll)]
        s = [[s_ref[ch["d"], ch["h0"] + j] for j in range(2)] for ch in chunks[0]]
        free = []

        def free_of(lo, hi):
            return state_free([ch for chains in chunks[lo:hi] for ch in chains], free)

        def steps(lo, hi):
            for j in range(lo, hi):
                yield from state_step(chunks[j], free[j * len(chunks[j]):(j + 1) * len(chunks[j])], s)

        bounds = list(range(0, unroll, GROUP)) + [unroll]
        alternate(free_of(bounds[0], bounds[1]), iter(()))
        for g in range(1, len(bounds) - 1):
            alternate(free_of(bounds[g], bounds[g + 1]), steps(bounds[g - 1], bounds[g]))
        alternate(steps(bounds[-2], bounds[-1]), iter(()))
        for ch, z in zip(chunks[0], s):
            for j in range(2):
                s_ref[ch["d"], ch["h0"] + j] = z[j]
        return carry

    lax.fori_loop(0, nch // unroll, body, 0)


def _delta_scan(qkv, gates, gates_t, k_t, tb):
    n, t, _ = qkv.shape
    nb = t // tb
    nch = tb // CHUNK
    fwd = lambda i: i
    bwd = lambda i: nb - 1 - i

    def specs(blk):
        return [pl.BlockSpec((None, tb, DN_DIM), lambda a, i: (a, blk(i), 0)),
                pl.BlockSpec((None, tb, DN_DIM), lambda a, i: (a, blk(i), 1)),
                pl.BlockSpec((None, tb, DN_DIM), lambda a, i: (a, blk(i), 2)),
                pl.BlockSpec((None, tb, LANES), lambda a, i: (a, blk(i), 0)),
                pl.BlockSpec((None, nch, N_GATE_COLS // 2, 2 * CHUNK), lambda a, i: (a, blk(i), 0, 0)),
                pl.BlockSpec((None, nch, DN_HEADS // 2, DN_HEAD_DIM, 2 * CHUNK), lambda a, i: (a, blk(i), 0, 0, 0))]

    out = jax.ShapeDtypeStruct((n, t, DN_DIM), BF16)
    args = (qkv, qkv, qkv, gates, gates_t, k_t)
    return pl.pallas_call(
        functools.partial(_delta_kernel, tb=tb),
        out_shape=(out, out),
        grid=(n, nb),
        in_specs=specs(fwd) + specs(bwd),
        out_specs=(pl.BlockSpec((None, tb, DN_DIM), lambda a, i: (a, fwd(i), 0)),
                   pl.BlockSpec((None, tb, DN_DIM), lambda a, i: (a, bwd(i), 0))),
        scratch_shapes=[pltpu.VMEM((2, DN_HEADS, DN_HEAD_DIM, DN_HEAD_DIM), F32)],
        compiler_params=_params(("parallel", "arbitrary")),
        name="delta_scan",
    )(*args, *args)


OUT_ROWS = 256


def _out_kernel(of_ref, ob_ref, zs_ref, ysc_ref, x_ref, w_ref, dnw_ref, npost_ref, nnext_ref,
                xo_ref, hn_ref, odn_ref, *, tm):
    def matmul(r0):
        rows = slice(r0, r0 + OUT_ROWS)
        for hh in range(DN_HEADS):
            sl = slice(hh * DN_HEAD_DIM, (hh + 1) * DN_HEAD_DIM)
            o = of_ref[rows, sl].astype(F32) + ob_ref[rows, sl].astype(F32)
            y = _rms(o) * dnw_ref[...] * zs_ref[rows, sl].astype(F32)
            odn_ref[rows, sl] = y.astype(odn_ref.dtype)
        return _bdot(odn_ref[rows, :], w_ref[0:DN_DIM, :]) + _bdot(ysc_ref[rows, :], w_ref[DN_DIM:, :])

    def epilogue(r0, m):
        rows = slice(r0, r0 + OUT_ROWS)
        xn = x_ref[rows, :] + _rms(m) * npost_ref[...]
        xo_ref[rows, :] = xn
        hn_ref[rows, :] = (_rms(xn) * nnext_ref[...]).astype(hn_ref.dtype)

    _pipelined([(functools.partial(matmul, r0), functools.partial(epilogue, r0))
                for r0 in range(0, tm, OUT_ROWS)])


def _out_proj(o_f, o_b, zs, ysc, x, w, layer, dn_w, n_post, n_next, tm):
    n, t, d = x.shape
    assert tm % OUT_ROWS == 0
    row = lambda width: pl.BlockSpec((None, tm, width), lambda a, i: (a, i, 0))
    vec = lambda width: pl.BlockSpec((1, width), lambda a, i: (0, 0))
    return pl.pallas_call(
        functools.partial(_out_kernel, tm=tm),
        out_shape=(jax.ShapeDtypeStruct((n, t, d), F32), jax.ShapeDtypeStruct((n, t, d), BF16)),
        grid=(n, t // tm),
        in_specs=[row(DN_DIM), row(DN_DIM), row(DN_DIM), row(SC_DIM), row(d),
                  _layer_resident(w, layer),
                  vec(DN_HEAD_DIM), vec(d), vec(d)],
        out_specs=(row(d), row(d)),
        scratch_shapes=[pltpu.VMEM((tm, DN_DIM), BF16)],
        compiler_params=_params(("parallel", "parallel")),
        name="out_proj",
    )(o_f, o_b, zs, ysc, x, w, dn_w.reshape(1, -1), n_post.reshape(1, d), n_next.reshape(1, d))


def _ffn_kernel(h_ref, hp_ref, hn_ref, wa_ref, wb_ref, cw_ref, wd_ref, x_ref, npost_ref, nnext_ref,
                xo_ref, *rest, tm, emit_next):
    ho_ref = rest[0] if emit_next else None
    hx_ref, acc_ref = rest[-2:]
    i = pl.program_id(1)
    f = pl.program_id(2)

    @pl.when(f == 0)
    def _():
        _fill_halo(hx_ref, h_ref, hp_ref, hn_ref, tm, i == 0, i == pl.num_programs(1) - 1)
        acc_ref[...] = jnp.zeros_like(acc_ref)

    a = _bdot(hx_ref[...], wa_ref[...])
    b = _bdot(h_ref[...], wb_ref[...])
    act = (_silu(_conv3_rows(a, cw_ref[...], tm)) * b).astype(BF16)
    acc_ref[...] += _bdot(act, wd_ref[...])

    @pl.when(f == pl.num_programs(2) - 1)
    def _():
        xn = x_ref[...] + _rms(acc_ref[...]) * npost_ref[...]
        xo_ref[...] = xn
        if emit_next:
            ho_ref[...] = (_rms(xn) * nnext_ref[...]).astype(ho_ref.dtype)


def _ffn(h, x, w_up, cw, w_down, layer, n_post, n_next, emit_next, tm, tf=512):
    n, t, d = x.shape
    nf = D_FF // tf
    main, prev, nxt = _halo_specs(tm, t, d)
    row = lambda: pl.BlockSpec((None, tm, d), lambda a, i, f: (a, i, 0))
    vec = lambda: pl.BlockSpec((1, d), lambda a, i, f: (0, 0))
    outs = (jax.ShapeDtypeStruct((n, t, d), F32), jax.ShapeDtypeStruct((n, t, d), BF16))[:2 if emit_next else 1]
    return pl.pallas_call(
        functools.partial(_ffn_kernel, tm=tm, emit_next=emit_next),
        out_shape=outs,
        grid=(n, t // tm, nf),
        in_specs=[main, prev, nxt,
                  pl.BlockSpec((None, d, tf), lambda a, i, f: (layer, 0, f)),
                  pl.BlockSpec((None, d, tf), lambda a, i, f: (layer, 0, f + nf)),
                  pl.BlockSpec((3, tf), lambda a, i, f: (0, f)),
                  pl.BlockSpec((None, tf, d), lambda a, i, f: (layer, f, 0)),
                  row(), vec(), vec()],
        out_specs=tuple(row() for _ in outs),
        scratch_shapes=[pltpu.VMEM((tm + 2 * HALO, d), BF16), pltpu.VMEM((tm, d), F32)],
        compiler_params=_params(("parallel", "parallel", "arbitrary")),
        name="ffn",
    )(h, h, h, w_up, w_up, cw, w_down, x, n_post.reshape(1, d), n_next.reshape(1, d))


def _tiles(t):
    return min(512, t), min(DELTA_UNROLL * CHUNK, t)


def _win_kernel(w_ref, q_ref, zg_ref, sc_ref):
    z_end = 4 * DN_DIM
    g_end = z_end + N_GATE_COLS
    q_ref[...] = w_ref[:, :3 * DN_DIM].astype(q_ref.dtype)
    zg_ref[:, :DN_DIM] = w_ref[:, 3 * DN_DIM:z_end].astype(zg_ref.dtype)
    gate = w_ref[:, z_end:z_end + LANES]
    lane = lax.broadcasted_iota(jnp.int32, gate.shape, 1)
    zg_ref[:, DN_DIM:] = jnp.where(lane < N_GATE_COLS, gate, 0.0).astype(zg_ref.dtype)
    gd = SC_GROUP_DIM
    for g in range(SC_GROUPS):
        for seg in range(3):
            src = g_end + seg * SC_DIM + g * gd
            sc_ref[:, (3 * g + seg) * gd:(3 * g + seg + 1) * gd] = w_ref[:, src:src + gd].astype(sc_ref.dtype)


def _prepare_w_in(w_in, tk=256):
    depth, d, cols = w_in.shape
    out = lambda width: jax.ShapeDtypeStruct((depth, d, width), BF16)
    spec = lambda width: pl.BlockSpec((None, tk, width), lambda l, i: (l, i, 0))
    return pl.pallas_call(
        _win_kernel,
        out_shape=(out(3 * DN_DIM), out(DN_DIM + LANES), out(3 * SC_DIM)),
        grid=(depth, d // tk),
        in_specs=[spec(cols)],
        out_specs=(spec(3 * DN_DIM), spec(DN_DIM + LANES), spec(3 * SC_DIM)),
        compiler_params=_params(("parallel", "parallel")),
        name="w_in_prep",
    )(w_in)


def _prepare(w_in, a_log, dt_bias):
    depth = w_in.shape[0]
    w_qkv, w_zg, w_sc = _prepare_w_in(w_in)
    vpad = jnp.zeros((depth, 1, LANES - 2 * DN_HEADS), F32)
    alog = jnp.concatenate([a_log.reshape(depth, 1, 2 * DN_HEADS).astype(F32), vpad], axis=-1)
    dtb = jnp.concatenate([dt_bias.reshape(depth, 1, 2 * DN_HEADS).astype(F32), vpad], axis=-1)
    return w_qkv, w_zg, w_sc, alog, dtb


def _trunk(x, prm):
    (norm_mix_pre, w_qkv, w_zg, w_sc, alog, dtb, conv_qkv, dn_norm, conv_sc, sc_norm, w_out,
     norm_mix_post, norm_ffn_pre, w_up, conv_ffn, w_down, norm_ffn_post) = prm
    depth = w_qkv.shape[0]
    n, t, _ = x.shape
    tm, tb = _tiles(t)
    h = x
    for l in range(depth):
        qkv, zs, gates, ysc, k_t, gates_t = _in_proj(h, norm_mix_pre[l], l == 0, w_qkv, w_zg, w_sc, l,
                                                     conv_qkv[l], conv_sc[l], sc_norm[l], alog[l], dtb[l], tm)
        gates_t = gates_t.reshape(n, t // CHUNK, N_GATE_COLS // 2, 2 * CHUNK)
        o_f, o_b = _delta_scan(qkv, gates, gates_t, k_t, tb)
        x, h = _out_proj(o_f, o_b, zs, ysc, x, w_out, l, dn_norm[l], norm_mix_post[l], norm_ffn_pre[l], tm)
        last = l == depth - 1
        x, *h = _ffn(h, x, w_up, conv_ffn[l], w_down, l, norm_ffn_post[l],
                     norm_mix_pre[0 if last else l + 1], not last, tm)
        h = h[0] if h else None
    return x


def kernel(x_prompt, x_sample, norm_mix_pre, w_in, conv_qkv, a_log, dt_bias, dn_norm, conv_sc, sc_norm, w_out, norm_mix_post, norm_ffn_pre, w_up, conv_ffn, w_down, norm_ffn_post):
    w_qkv, w_zg, w_sc, alog, dtb = _prepare(w_in, a_log, dt_bias)
    prm = (norm_mix_pre, w_qkv, w_zg, w_sc, alog, dtb, conv_qkv, dn_norm, conv_sc, sc_norm,
           w_out.astype(BF16), norm_mix_post, norm_ffn_pre, w_up.astype(BF16), conv_ffn,
           w_down.astype(BF16), norm_ffn_post)
    return (_trunk(x_prompt, prm), _trunk(x_sample, prm))
```

```python
import functools
import math

import jax
import jax.numpy as jnp
from jax import lax
from jax.experimental import pallas as pl
from jax.experimental.pallas import tpu as pltpu

D_MODEL = 2048
DN_HEADS = 8
DN_HEAD_DIM = 128
DN_DIM = DN_HEADS * DN_HEAD_DIM
SC_GROUPS = 8
SC_GROUP_DIM = 128
SC_DIM = SC_GROUPS * SC_GROUP_DIM
D_FF = 5632
CHUNK = 64
N_GATE_COLS = 4 * DN_HEADS
NORM_EPS = 1e-6
L2_EPS = 1e-6
Q_SCALE = DN_HEAD_DIM ** -0.5

LANES = 128
HALO = 16
VMEM_LIMIT = 56 * 1024 * 1024

F32 = jnp.float32
BF16 = jnp.bfloat16
NT_DIMS = (((1,), (1,)), ((), ()))
TN_DIMS = (((0,), (0,)), ((), ()))


def _sigmoid(x):
    return 1.0 / (1.0 + jnp.exp(-x))


def _silu(x):
    return x * _sigmoid(x)


def _softplus(x):
    return jnp.maximum(x, 0.0) + jnp.log1p(jnp.exp(-jnp.abs(x)))


def _rms(x):
    return x * lax.rsqrt(jnp.mean(x * x, axis=-1, keepdims=True) + NORM_EPS)


def _bdot(a, b):
    return jnp.dot(a, b, preferred_element_type=F32)


def _conv3_rows(p, cw, tm):
    rows = p.shape[0]
    prev = pltpu.roll(p, 1, 0)[HALO:HALO + tm]
    nxt = pltpu.roll(p, rows - 1, 0)[HALO:HALO + tm]
    mid = p[HALO:HALO + tm]
    return prev * cw[0:1] + mid * cw[1:2] + nxt * cw[2:3]


def _fill_halo(hx_ref, h_ref, hp_ref, hn_ref, tm, first, last):
    hp = hp_ref[...]
    hn = hn_ref[...]
    hx_ref[0:HALO, :] = jnp.where(first, jnp.zeros_like(hp), hp)
    hx_ref[HALO:HALO + tm, :] = h_ref[...]
    hx_ref[HALO + tm:HALO + tm + HALO, :] = jnp.where(last, jnp.zeros_like(hn), hn)


def _with_halo(h, hp, hn, first, last):
    return jnp.concatenate([jnp.where(first, jnp.zeros_like(hp), hp), h,
                            jnp.where(last, jnp.zeros_like(hn), hn)], axis=0)


def _halo_specs(tm, t_len, width):
    per = tm // HALO
    nblk = t_len // HALO
    main = pl.BlockSpec((None, tm, width), lambda n, i, *_: (n, i, 0))
    prev = pl.BlockSpec((None, HALO, width), lambda n, i, *_: (n, jnp.maximum(i * per - 1, 0), 0))
    nxt = pl.BlockSpec((None, HALO, width), lambda n, i, *_: (n, jnp.minimum((i + 1) * per, nblk - 1), 0))
    return main, prev, nxt


def _params(sem):
    return pltpu.CompilerParams(dimension_semantics=sem, vmem_limit_bytes=VMEM_LIMIT)


QKV_TILE = 512
SC_TILE_GROUPS = 2


def _pipelined(tasks):
    pending = None
    for matmul, epilogue in tasks:
        val = matmul()
        if pending is not None:
            pending[0](pending[1])
        pending = (epilogue, val)
    pending[0](pending[1])


def _split3(x):
    hi = x.astype(BF16)
    r1 = x - hi.astype(F32)
    mid = r1.astype(BF16)
    lo = (r1 - mid.astype(F32)).astype(BF16)
    return hi, mid, lo


def _inproj_kernel(h_ref, hp_ref, hn_ref, nw_ref, wq_ref, wz_ref, ws_ref, cwq_ref, cws_ref, gain_ref, alog_ref,
                   dtb_ref, qkv_ref, z_ref, g_ref, ysc_ref, kt_ref, gt_ref, *, tm):
    i = pl.program_id(1)

    def rows_of(ref):
        return (_rms(ref[...]) * nw_ref[...]).astype(BF16)

    h = rows_of(h_ref)
    hx = _with_halo(h, rows_of(hp_ref), rows_of(hn_ref), i == 0, i == pl.num_programs(1) - 1)
    gd = SC_GROUP_DIM
    sc_tile = SC_TILE_GROUPS * 3 * gd

    def halo_matmul(w_ref, c0, width):
        return _bdot(hx, w_ref[:, c0:c0 + width])

    def plain_matmul(w_ref, c0, width):
        return _bdot(h, w_ref[:, c0:c0 + width])

    def gate_epilogue(gl):
        g = -jnp.exp(alog_ref[...]) * _softplus(gl + dtb_ref[...])
        beta = _sigmoid(gl)
        hi, mid, lo = _split3(g)
        r = lax.broadcasted_iota(jnp.int32, (CHUNK, 3 * CHUNK), 0)
        c = lax.broadcasted_iota(jnp.int32, (CHUNK, 3 * CHUNK), 1) % CHUNK
        m_fwd = (c <= r).astype(BF16)
        m_bwd = (c >= r).astype(BF16)
        lane = lax.broadcasted_iota(jnp.int32, (CHUNK, LANES), 1)
        for ci in range(tm // CHUNK):
            rows = slice(ci * CHUNK, (ci + 1) * CHUNK)
            g3 = jnp.concatenate([hi[rows], mid[rows], lo[rows]], axis=0)
            cum_f = _bdot(m_fwd, g3)
            cum_b = _bdot(m_bwd, g3)
            tile = jnp.where(lane < DN_HEADS, cum_f, jnp.where(lane < 2 * DN_HEADS, cum_b, beta[rows]))
            g_ref[rows, :] = tile
            gt_ref[ci] = tile.T[:N_GATE_COLS, :]

    def qkv_epilogue(c0, p):
        y = _silu(_conv3_rows(p, cwq_ref[:, c0:c0 + QKV_TILE], tm))
        if c0 >= 2 * DN_DIM:
            qkv_ref[:, c0:c0 + QKV_TILE] = y.astype(qkv_ref.dtype)
            return
        heads = []
        for hh in range(QKV_TILE // DN_HEAD_DIM):
            yh = y[:, hh * DN_HEAD_DIM:(hh + 1) * DN_HEAD_DIM]
            ss = jnp.sum(yh * yh, axis=-1, keepdims=True)
            sl = slice(c0 + hh * DN_HEAD_DIM, c0 + (hh + 1) * DN_HEAD_DIM)
            heads.append(yh * lax.rsqrt(ss + L2_EPS))
            qkv_ref[:, sl] = heads[-1].astype(qkv_ref.dtype)
        if c0 >= DN_DIM:
            pair0 = (c0 - DN_DIM) // (2 * DN_HEAD_DIM)
            for ci in range(tm // CHUNK):
                rows = slice(ci * CHUNK, (ci + 1) * CHUNK)
                for pp in range(len(heads) // 2):
                    both = jnp.concatenate([heads[2 * pp][rows], heads[2 * pp + 1][rows]], axis=0)
                    kt_ref[ci, pair0 + pp] = both.T.astype(kt_ref.dtype)

    def z_epilogue(c0, p):
        z_ref[:, c0:c0 + p.shape[1]] = _silu(p).astype(z_ref.dtype)

    def sc_epilogue(jn, p):
        for gi in range(SC_TILE_GROUPS):
            base = gi * 3 * gd
            b = p[HALO:HALO + tm, base:base + gd]
            cx = p[:, base + gd:base + 2 * gd] * p[:, base + 2 * gd:base + 3 * gd]
            g0 = (jn * SC_TILE_GROUPS + gi) * gd
            y = b * _conv3_rows(cx, cws_ref[:, g0:g0 + gd], tm)
            ysc_ref[:, g0:g0 + gd] = (_rms(y) * gain_ref[:, g0:g0 + gd]).astype(ysc_ref.dtype)

    part = functools.partial
    zq = QKV_TILE // 2

    def z_gate_epilogue(p):
        z_epilogue(DN_DIM - zq, p[:, :zq])
        gate_epilogue(p[:, zq:])

    tasks = [(part(plain_matmul, wz_ref, DN_DIM - zq, zq + LANES), z_gate_epilogue)]
    tasks += [(part(halo_matmul, wq_ref, c0, QKV_TILE), part(qkv_epilogue, c0))
              for c0 in range(0, 3 * DN_DIM, QKV_TILE)]
    tasks += [(part(plain_matmul, wz_ref, DN_DIM - 2 * zq, zq), part(z_epilogue, DN_DIM - 2 * zq))]
    tasks += [(part(halo_matmul, ws_ref, jn * sc_tile, sc_tile), part(sc_epilogue, jn))
              for jn in range(SC_GROUPS // SC_TILE_GROUPS)]
    tasks += [(part(plain_matmul, wz_ref, c0, QKV_TILE), part(z_epilogue, c0))
              for c0 in range(0, DN_DIM - 2 * zq, QKV_TILE)]
    _pipelined(tasks)


def _layer_resident(w, layer):
    return pl.BlockSpec((None,) + w.shape[1:], lambda *_: (layer, 0, 0), pipeline_mode=pl.Buffered(1))


def _in_proj(x, n_pre, w_qkv, w_zg, w_sc, layer, cw_qkv, cw_sc, gain, alog, dtb, tm):
    n, t, d = x.shape
    main, prev, nxt = _halo_specs(tm, t, d)
    whole = lambda shape: pl.BlockSpec(shape, lambda a, i: (0, 0))
    row = lambda width: pl.BlockSpec((None, tm, width), lambda a, i: (a, i, 0))
    return pl.pallas_call(
        functools.partial(_inproj_kernel, tm=tm),
        out_shape=(jax.ShapeDtypeStruct((n, t, 3 * DN_DIM), BF16),
                   jax.ShapeDtypeStruct((n, t, DN_DIM), BF16),
                   jax.ShapeDtypeStruct((n, t, LANES), F32),
                   jax.ShapeDtypeStruct((n, t, SC_DIM), BF16),
                   jax.ShapeDtypeStruct((n, t // CHUNK, DN_HEADS // 2, DN_HEAD_DIM, 2 * CHUNK), BF16),
                   jax.ShapeDtypeStruct((n, t // CHUNK, N_GATE_COLS, CHUNK), F32)),
        grid=(n, t // tm),
        in_specs=[main, prev, nxt, whole((1, d)),
                  _layer_resident(w_qkv, layer), _layer_resident(w_zg, layer), _layer_resident(w_sc, layer),
                  whole((3, 3 * DN_DIM)), whole((3, SC_DIM)), whole((1, SC_DIM)),
                  whole((1, LANES)), whole((1, LANES))],
        out_specs=(row(3 * DN_DIM), row(DN_DIM), row(LANES), row(SC_DIM),
                   pl.BlockSpec((None, tm // CHUNK, DN_HEADS // 2, DN_HEAD_DIM, 2 * CHUNK),
                                lambda a, i: (a, i, 0, 0, 0)),
                   pl.BlockSpec((None, tm // CHUNK, N_GATE_COLS, CHUNK), lambda a, i: (a, i, 0, 0))),
        compiler_params=_params(("parallel", "parallel")),
        name="in_proj",
    )(x, x, x, n_pre.reshape(1, d), w_qkv, w_zg, w_sc, cw_qkv, cw_sc, gain.reshape(1, SC_DIM), alog, dtb)


DELTA_UNROLL = 8
GROUP = 2


def _delta_kernel(qf_ref, kf_ref, vf_ref, gcf_ref, grf_ref, ktf_ref, qb_ref, kb_ref, vb_ref, gcb_ref, grb_ref,
                  ktb_ref, of_ref, ob_ref, s_ref, *, tb):
    @pl.when(pl.program_id(1) == 0)
    def _():
        s_ref[...] = jnp.zeros_like(s_ref)

    nch = tb // CHUNK
    unroll = math.gcd(nch, DELTA_UNROLL)
    pair_w = 2 * DN_HEAD_DIM
    r = lax.broadcasted_iota(jnp.int32, (CHUNK, 2 * CHUNK), 0)
    lane = lax.broadcasted_iota(jnp.int32, (CHUNK, 2 * CHUNK), 1)
    c = lane % CHUNK
    second = lane >= CHUNK
    eye = (r == c).astype(F32)
    masks = ((r >= c, r > c), (r <= c, r < c))
    row_head = lambda width: lax.broadcasted_iota(jnp.int32, (2 * CHUNK, width), 0) // CHUNK
    col_head = lambda width, per: lax.broadcasted_iota(jnp.int32, (2 * CHUNK, width), 1) // per
    diag_sq = (row_head(2 * CHUNK) == col_head(2 * CHUNK, CHUNK)).astype(BF16)
    diag_wide = (row_head(pair_w) == col_head(pair_w, DN_HEAD_DIM)).astype(BF16)
    zero_s = jnp.zeros((DN_HEAD_DIM, DN_HEAD_DIM), BF16)
    refs = ((qf_ref, kf_ref, vf_ref, gcf_ref, grf_ref, ktf_ref, of_ref),
            (qb_ref, kb_ref, vb_ref, gcb_ref, grb_ref, ktb_ref, ob_ref))

    def ld(ch, name):
        return ch[name + "_ref"][ch["rows"], ch["sl"]]

    def block_diag(x, sel):
        return jnp.concatenate([x, x], axis=0) * sel

    def pair_cols(x, h0):
        shape = (CHUNK, 2 * CHUNK)
        return jnp.where(second, jnp.broadcast_to(x[:, h0 + 1:h0 + 2], shape), jnp.broadcast_to(x[:, h0:h0 + 1], shape))

    def chains_of(jj):
        chains = []
        for d, (q_ref, k_ref, v_ref, gc_ref, gr_ref, kt_ref, o_ref) in enumerate(refs):
            cc = (nch - 1 - jj) if d else jj
            rows = pl.ds(pl.multiple_of(cc * CHUNK, CHUNK), CHUNK)
            gcol = gc_ref[rows, :]
            grow = gr_ref[cc]
            goff = d * DN_HEADS
            boff = 2 * DN_HEADS + goff
            cum_c = gcol[:, goff:goff + DN_HEADS]
            b_c = gcol[:, boff:boff + DN_HEADS]
            last = 0 if d else CHUNK - 1
            gl_c = cum_c[last:last + 1, :]
            eg_c = jnp.exp(cum_c) * Q_SCALE
            ekd_c = jnp.exp(gl_c - cum_c)
            egl = jnp.exp(gl_c)
            npair = DN_HEADS // 2
            for p in range(npair):
                h0 = 2 * p
                gr2 = grow[d * npair + p:d * npair + p + 1, :]
                br2 = grow[(2 + d) * npair + p:(2 + d) * npair + p + 1, :]
                chains.append(dict(
                    d=d, h0=h0, rows=rows, sl=slice(p * pair_w, (p + 1) * pair_w),
                    q_ref=q_ref, k_ref=k_ref, v_ref=v_ref, o_ref=o_ref, mle=masks[d][0], strict=masks[d][1],
                    kt_ref=kt_ref, cc=cc, p=p,
                    gc=pair_cols(cum_c, h0), bc=pair_cols(b_c, h0), gr=gr2, br=br2, ebr=br2 * jnp.exp(gr2),
                    egc=[eg_c[:, h0 + j:h0 + j + 1] for j in range(2)],
                    ekd=[ekd_c[:, h0 + j:h0 + j + 1] for j in range(2)],
                    egl=[egl[:, h0 + j:h0 + j + 1] for j in range(2)]))
        return chains

    def head(x, j):
        return x[:, j * DN_HEAD_DIM:(j + 1) * DN_HEAD_DIM]

    def bd_sq(x):
        return block_diag(x.astype(BF16), diag_sq)

    def state_free(chains, out):
        decay = [jnp.where(ch["mle"], jnp.exp(jnp.where(ch["mle"], ch["gc"] - ch["gr"], 0.0)), 0.0)
                 for ch in chains]
        bdk = [block_diag(ld(ch, "k"), diag_wide) for ch in chains]
        kq = [lax.dot_general(jnp.concatenate([ld(ch, "k"), ld(ch, "q")], axis=0), x, NT_DIMS,
                              preferred_element_type=F32) for ch, x in zip(chains, bdk)]
        yield
        a_qk = [(x[CHUNK:] * dc * Q_SCALE).astype(BF16) for x, dc in zip(kq, decay)]
        pw = [jnp.where(ch["strict"], -(x[:CHUNK] * dc) * ch["bc"], 0.0) for ch, x, dc in zip(chains, kq, decay)]
        tinv = [eye + p for p in pw]
        pw = [_bdot(p.astype(BF16), bd_sq(p)) for p in pw]
        yield
        for _ in range(4):
            both = [_bdot(jnp.concatenate([t, p], axis=0).astype(BF16), bd_sq(p)) for t, p in zip(tinv, pw)]
            tinv = [t + b[:CHUNK] for t, b in zip(tinv, both)]
            pw = [b[CHUNK:] for b in both]
            yield
        tinv = [t + _bdot(t.astype(BF16), bd_sq(p)) for t, p in zip(tinv, pw)]
        yield
        bdv = [block_diag(ld(ch, "v"), diag_wide) for ch in chains]
        u = [_bdot((t * ch["br"]).astype(BF16), x) for ch, t, x in zip(chains, tinv, bdv)]
        w = [_bdot((t * ch["ebr"]).astype(BF16), x) for ch, t, x in zip(chains, tinv, bdk)]
        qg = [jnp.concatenate([(head(ld(ch, "q"), j).astype(F32) * ch["egc"][j]).astype(BF16) for j in range(2)],
                              axis=1) for ch in chains]
        out.extend(zip(a_qk, u, w, qg))

    def state_step(chains, free, s):
        s_bd = [jnp.concatenate([jnp.concatenate([z[0].astype(BF16), zero_s], axis=1),
                                 jnp.concatenate([zero_s, z[1].astype(BF16)], axis=1)], axis=0) for z in s]
        wq = [_bdot(jnp.concatenate([w.astype(BF16), qg], axis=0), z)
              for (_, _, w, qg), z in zip(free, s_bd)]
        yield
        vn = [u - y[:CHUNK] for (_, u, _, _), y in zip(free, wq)]
        for ch, y, (a_qk, _, _, _), x in zip(chains, wq, free, vn):
            o = y[CHUNK:] + _bdot(a_qk, block_diag(x.astype(BF16), diag_wide))
            ch["o_ref"][ch["rows"], ch["sl"]] = o.astype(ch["o_ref"].dtype)
        vd = [jnp.concatenate([(head(x, j) * ch["ekd"][j]).astype(BF16) for j in range(2)], axis=1)
              for ch, x in zip(chains, vn)]
        upd = [_bdot(ch["kt_ref"][ch["cc"], ch["p"]], block_diag(x, diag_wide)) for ch, x in zip(chains, vd)]
        yield
        s[:] = [[z[j] * ch["egl"][j] + head(y, j) for j in range(2)] for ch, z, y in zip(chains, s, upd)]

    def alternate(first, second):
        for _ in first:
            next(second, None)
        for _ in second:
            pass

    def body(it, carry):
        chunks = [chains_of(it * unroll + j) for j in range(unroll)]
        s = [[s_ref[ch["d"], ch["h0"] + j] for j in range(2)] for ch in chunks[0]]
        free = []

        def free_of(lo, hi):
            return state_free([ch for chains in chunks[lo:hi] for ch in chains], free)

        def steps(lo, hi):
            for j in range(lo, hi):
                yield from state_step(chunks[j], free[j * len(chunks[j]):(j + 1) * len(chunks[j])], s)

        bounds = list(range(0, unroll, GROUP)) + [unroll]
        alternate(free_of(bounds[0], bounds[1]), iter(()))
        for g in range(1, len(bounds) - 1):
            alternate(free_of(bounds[g], bounds[g + 1]), steps(bounds[g - 1], bounds[g]))
        alternate(steps(bounds[-2], bounds[-1]), iter(()))
        for ch, z in zip(chunks[0], s):
            for j in range(2):
                s_ref[ch["d"], ch["h0"] + j] = z[j]
        return carry

    lax.fori_loop(0, nch // unroll, body, 0)


def _delta_scan(qkv, gates, gates_t, k_t, tb):
    n, t, _ = qkv.shape
    nb = t // tb
    nch = tb // CHUNK
    fwd = lambda i: i
    bwd = lambda i: nb - 1 - i

    def specs(blk):
        return [pl.BlockSpec((None, tb, DN_DIM), lambda a, i: (a, blk(i), 0)),
                pl.BlockSpec((None, tb, DN_DIM), lambda a, i: (a, blk(i), 1)),
                pl.BlockSpec((None, tb, DN_DIM), lambda a, i: (a, blk(i), 2)),
                pl.BlockSpec((None, tb, LANES), lambda a, i: (a, blk(i), 0)),
                pl.BlockSpec((None, nch, N_GATE_COLS // 2, 2 * CHUNK), lambda a, i: (a, blk(i), 0, 0)),
                pl.BlockSpec((None, nch, DN_HEADS // 2, DN_HEAD_DIM, 2 * CHUNK), lambda a, i: (a, blk(i), 0, 0, 0))]

    out = jax.ShapeDtypeStruct((n, t, DN_DIM), BF16)
    args = (qkv, qkv, qkv, gates, gates_t, k_t)
    return pl.pallas_call(
        functools.partial(_delta_kernel, tb=tb),
        out_shape=(out, out),
        grid=(n, nb),
        in_specs=specs(fwd) + specs(bwd),
        out_specs=(pl.BlockSpec((None, tb, DN_DIM), lambda a, i: (a, fwd(i), 0)),
                   pl.BlockSpec((None, tb, DN_DIM), lambda a, i: (a, bwd(i), 0))),
        scratch_shapes=[pltpu.VMEM((2, DN_HEADS, DN_HEAD_DIM, DN_HEAD_DIM), F32)],
        compiler_params=_params(("parallel", "arbitrary")),
        name="delta_scan",
    )(*args, *args)


OUT_ROWS = 256


def _out_kernel(of_ref, ob_ref, zs_ref, ysc_ref, x_ref, w_ref, dnw_ref, npost_ref, nnext_ref,
                xo_ref, hn_ref, odn_ref, *, tm):
    def matmul(r0):
        rows = slice(r0, r0 + OUT_ROWS)
        for hh in range(DN_HEADS):
            sl = slice(hh * DN_HEAD_DIM, (hh + 1) * DN_HEAD_DIM)
            o = of_ref[rows, sl].astype(F32) + ob_ref[rows, sl].astype(F32)
            y = _rms(o) * dnw_ref[...] * zs_ref[rows, sl].astype(F32)
            odn_ref[rows, sl] = y.astype(odn_ref.dtype)
        return _bdot(odn_ref[rows, :], w_ref[0:DN_DIM, :]) + _bdot(ysc_ref[rows, :], w_ref[DN_DIM:, :])

    def epilogue(r0, m):
        rows = slice(r0, r0 + OUT_ROWS)
        xn = x_ref[rows, :] + _rms(m) * npost_ref[...]
        xo_ref[rows, :] = xn
        hn_ref[rows, :] = (_rms(xn) * nnext_ref[...]).astype(hn_ref.dtype)

    _pipelined([(functools.partial(matmul, r0), functools.partial(epilogue, r0))
                for r0 in range(0, tm, OUT_ROWS)])


def _out_proj(o_f, o_b, zs, ysc, x, w, layer, dn_w, n_post, n_next, tm):
    n, t, d = x.shape
    assert tm % OUT_ROWS == 0
    row = lambda width: pl.BlockSpec((None, tm, width), lambda a, i: (a, i, 0))
    vec = lambda width: pl.BlockSpec((1, width), lambda a, i: (0, 0))
    return pl.pallas_call(
        functools.partial(_out_kernel, tm=tm),
        out_shape=(jax.ShapeDtypeStruct((n, t, d), F32), jax.ShapeDtypeStruct((n, t, d), BF16)),
        grid=(n, t // tm),
        in_specs=[row(DN_DIM), row(DN_DIM), row(DN_DIM), row(SC_DIM), row(d),
                  _layer_resident(w, layer),
                  vec(DN_HEAD_DIM), vec(d), vec(d)],
        out_specs=(row(d), row(d)),
        scratch_shapes=[pltpu.VMEM((tm, DN_DIM), BF16)],
        compiler_params=_params(("parallel", "parallel")),
        name="out_proj",
    )(o_f, o_b, zs, ysc, x, w, dn_w.reshape(1, -1), n_post.reshape(1, d), n_next.reshape(1, d))


def _ffn_kernel(h_ref, hp_ref, hn_ref, wa_ref, wb_ref, cw_ref, wd_ref, x_ref, npost_ref,
                xo_ref, hx_ref, acc_ref, *, tm):
    i = pl.program_id(1)
    f = pl.program_id(2)

    @pl.when(f == 0)
    def _():
        _fill_halo(hx_ref, h_ref, hp_ref, hn_ref, tm, i == 0, i == pl.num_programs(1) - 1)
        acc_ref[...] = jnp.zeros_like(acc_ref)

    a = _bdot(hx_ref[...], wa_ref[...])
    b = _bdot(h_ref[...], wb_ref[...])
    act = (_silu(_conv3_rows(a, cw_ref[...], tm)) * b).astype(BF16)
    acc_ref[...] += _bdot(act, wd_ref[...])

    @pl.when(f == pl.num_programs(2) - 1)
    def _():
        xo_ref[...] = x_ref[...] + _rms(acc_ref[...]) * npost_ref[...]


def _ffn(h, x, w_up, cw, w_down, layer, n_post, tm, tf=512):
    n, t, d = x.shape
    nf = D_FF // tf
    main, prev, nxt = _halo_specs(tm, t, d)
    row = lambda: pl.BlockSpec((None, tm, d), lambda a, i, f: (a, i, 0))
    vec = lambda: pl.BlockSpec((1, d), lambda a, i, f: (0, 0))
    return pl.pallas_call(
        functools.partial(_ffn_kernel, tm=tm),
        out_shape=jax.ShapeDtypeStruct((n, t, d), F32),
        grid=(n, t // tm, nf),
        in_specs=[main, prev, nxt,
                  pl.BlockSpec((None, d, tf), lambda a, i, f: (layer, 0, f)),
                  pl.BlockSpec((None, d, tf), lambda a, i, f: (layer, 0, f + nf)),
                  pl.BlockSpec((3, tf), lambda a, i, f: (0, f)),
                  pl.BlockSpec((None, tf, d), lambda a, i, f: (layer, f, 0)),
                  row(), vec()],
        out_specs=row(),
        scratch_shapes=[pltpu.VMEM((tm + 2 * HALO, d), BF16), pltpu.VMEM((tm, d), F32)],
        compiler_params=_params(("parallel", "parallel", "arbitrary")),
        name="ffn",
    )(h, h, h, w_up, w_up, cw, w_down, x, n_post.reshape(1, d))


def _tiles(t):
    return min(512, t), min(DELTA_UNROLL * CHUNK, t)


def _win_kernel(w_ref, q_ref, zg_ref, sc_ref):
    z_end = 4 * DN_DIM
    g_end = z_end + N_GATE_COLS
    q_ref[...] = w_ref[:, :3 * DN_DIM].astype(q_ref.dtype)
    zg_ref[:, :DN_DIM] = w_ref[:, 3 * DN_DIM:z_end].astype(zg_ref.dtype)
    gate = w_ref[:, z_end:z_end + LANES]
    lane = lax.broadcasted_iota(jnp.int32, gate.shape, 1)
    zg_ref[:, DN_DIM:] = jnp.where(lane < N_GATE_COLS, gate, 0.0).astype(zg_ref.dtype)
    gd = SC_GROUP_DIM
    for g in range(SC_GROUPS):
        for seg in range(3):
            src = g_end + seg * SC_DIM + g * gd
            sc_ref[:, (3 * g + seg) * gd:(3 * g + seg + 1) * gd] = w_ref[:, src:src + gd].astype(sc_ref.dtype)


def _prepare_w_in(w_in, tk=256):
    depth, d, cols = w_in.shape
    out = lambda width: jax.ShapeDtypeStruct((depth, d, width), BF16)
    spec = lambda width: pl.BlockSpec((None, tk, width), lambda l, i: (l, i, 0))
    return pl.pallas_call(
        _win_kernel,
        out_shape=(out(3 * DN_DIM), out(DN_DIM + LANES), out(3 * SC_DIM)),
        grid=(depth, d // tk),
        in_specs=[spec(cols)],
        out_specs=(spec(3 * DN_DIM), spec(DN_DIM + LANES), spec(3 * SC_DIM)),
        compiler_params=_params(("parallel", "parallel")),
        name="w_in_prep",
    )(w_in)


def _prepare(w_in, a_log, dt_bias):
    depth = w_in.shape[0]
    w_qkv, w_zg, w_sc = _prepare_w_in(w_in)
    vpad = jnp.zeros((depth, 1, LANES - 2 * DN_HEADS), F32)
    alog = jnp.concatenate([a_log.reshape(depth, 1, 2 * DN_HEADS).astype(F32), vpad], axis=-1)
    dtb = jnp.concatenate([dt_bias.reshape(depth, 1, 2 * DN_HEADS).astype(F32), vpad], axis=-1)
    return w_qkv, w_zg, w_sc, alog, dtb


def _trunk(x, prm):
    (norm_mix_pre, w_qkv, w_zg, w_sc, alog, dtb, conv_qkv, dn_norm, conv_sc, sc_norm, w_out,
     norm_mix_post, norm_ffn_pre, w_up, conv_ffn, w_down, norm_ffn_post) = prm
    depth = w_qkv.shape[0]
    n, t, _ = x.shape
    tm, tb = _tiles(t)
    for l in range(depth):
        qkv, zs, gates, ysc, k_t, gates_t = _in_proj(x, norm_mix_pre[l], w_qkv, w_zg, w_sc, l,
                                                     conv_qkv[l], conv_sc[l], sc_norm[l], alog[l], dtb[l], tm)
        gates_t = gates_t.reshape(n, t // CHUNK, N_GATE_COLS // 2, 2 * CHUNK)
        o_f, o_b = _delta_scan(qkv, gates, gates_t, k_t, tb)
        x, h = _out_proj(o_f, o_b, zs, ysc, x, w_out, l, dn_norm[l], norm_mix_post[l], norm_ffn_pre[l], tm)
        x = _ffn(h, x, w_up, conv_ffn[l], w_down, l, norm_ffn_post[l], tm)
    return x


def kernel(x_prompt, x_sample, norm_mix_pre, w_in, conv_qkv, a_log, dt_bias, dn_norm, conv_sc, sc_norm, w_out, norm_mix_post, norm_ffn_pre, w_up, conv_ffn, w_down, norm_ffn_post):
    w_qkv, w_zg, w_sc, alog, dtb = _prepare(w_in, a_log, dt_bias)
    prm = (norm_mix_pre, w_qkv, w_zg, w_sc, alog, dtb, conv_qkv, dn_norm, conv_sc, sc_norm,
           w_out.astype(BF16), norm_mix_post, norm_ffn_pre, w_up.astype(BF16), conv_ffn,
           w_down.astype(BF16), norm_ffn_post)
    return (_trunk(x_prompt, prm), _trunk(x_sample, prm))
```

```python
import functools
import math

import jax
import jax.numpy as jnp
from jax import lax
from jax.experimental import pallas as pl
from jax.experimental.pallas import tpu as pltpu

D_MODEL = 2048
DN_HEADS = 8
DN_HEAD_DIM = 128
DN_DIM = DN_HEADS * DN_HEAD_DIM
SC_GROUPS = 8
SC_GROUP_DIM = 128
SC_DIM = SC_GROUPS * SC_GROUP_DIM
D_FF = 5632
CHUNK = 64
N_GATE_COLS = 4 * DN_HEADS
NORM_EPS = 1e-6
L2_EPS = 1e-6
Q_SCALE = DN_HEAD_DIM ** -0.5

LANES = 128
HALO = 16
VMEM_LIMIT = 56 * 1024 * 1024

F32 = jnp.float32
BF16 = jnp.bfloat16
NT_DIMS = (((1,), (1,)), ((), ()))
TN_DIMS = (((0,), (0,)), ((), ()))


def _sigmoid(x):
    return 1.0 / (1.0 + jnp.exp(-x))


def _silu(x):
    return x * _sigmoid(x)


def _softplus(x):
    return jnp.maximum(x, 0.0) + jnp.log1p(jnp.exp(-jnp.abs(x)))


def _rms(x):
    return x * lax.rsqrt(jnp.mean(x * x, axis=-1, keepdims=True) + NORM_EPS)


def _bdot(a, b):
    return jnp.dot(a, b, preferred_element_type=F32)


def _conv3_rows(p, cw, tm):
    rows = p.shape[0]
    prev = pltpu.roll(p, 1, 0)[HALO:HALO + tm]
    nxt = pltpu.roll(p, rows - 1, 0)[HALO:HALO + tm]
    mid = p[HALO:HALO + tm]
    return prev * cw[0:1] + mid * cw[1:2] + nxt * cw[2:3]


def _fill_halo(hx_ref, h_ref, hp_ref, hn_ref, tm, first, last):
    hp = hp_ref[...]
    hn = hn_ref[...]
    hx_ref[0:HALO, :] = jnp.where(first, jnp.zeros_like(hp), hp)
    hx_ref[HALO:HALO + tm, :] = h_ref[...]
    hx_ref[HALO + tm:HALO + tm + HALO, :] = jnp.where(last, jnp.zeros_like(hn), hn)


def _with_halo(h, hp, hn, first, last):
    return jnp.concatenate([jnp.where(first, jnp.zeros_like(hp), hp), h,
                            jnp.where(last, jnp.zeros_like(hn), hn)], axis=0)


def _halo_specs(tm, t_len, width):
    per = tm // HALO
    nblk = t_len // HALO
    main = pl.BlockSpec((None, tm, width), lambda n, i, *_: (n, i, 0))
    prev = pl.BlockSpec((None, HALO, width), lambda n, i, *_: (n, jnp.maximum(i * per - 1, 0), 0))
    nxt = pl.BlockSpec((None, HALO, width), lambda n, i, *_: (n, jnp.minimum((i + 1) * per, nblk - 1), 0))
    return main, prev, nxt


def _params(sem):
    return pltpu.CompilerParams(dimension_semantics=sem, vmem_limit_bytes=VMEM_LIMIT)


QKV_TILE = 512
SC_TILE_GROUPS = 2


def _pipelined(tasks):
    pending = None
    for matmul, epilogue in tasks:
        val = matmul()
        if pending is not None:
            pending[0](pending[1])
        pending = (epilogue, val)
    pending[0](pending[1])


def _split3(x):
    hi = x.astype(BF16)
    r1 = x - hi.astype(F32)
    mid = r1.astype(BF16)
    lo = (r1 - mid.astype(F32)).astype(BF16)
    return hi, mid, lo


def _inproj_kernel(h_ref, hp_ref, hn_ref, nw_ref, wq_ref, wz_ref, ws_ref, cwq_ref, cws_ref, gain_ref, alog_ref,
                   dtb_ref, qkv_ref, z_ref, g_ref, ysc_ref, kt_ref, gt_ref, *, tm):
    i = pl.program_id(1)

    def rows_of(ref):
        return (_rms(ref[...]) * nw_ref[...]).astype(BF16)

    h = rows_of(h_ref)
    hx = _with_halo(h, rows_of(hp_ref), rows_of(hn_ref), i == 0, i == pl.num_programs(1) - 1)
    gd = SC_GROUP_DIM
    sc_tile = SC_TILE_GROUPS * 3 * gd

    def halo_matmul(w_ref, c0, width):
        return _bdot(hx, w_ref[:, c0:c0 + width])

    def plain_matmul(w_ref, c0, width):
        return _bdot(h, w_ref[:, c0:c0 + width])

    def gate_epilogue(gl):
        g = -jnp.exp(alog_ref[...]) * _softplus(gl + dtb_ref[...])
        beta = _sigmoid(gl)
        hi, mid, lo = _split3(g)
        r = lax.broadcasted_iota(jnp.int32, (CHUNK, 3 * CHUNK), 0)
        c = lax.broadcasted_iota(jnp.int32, (CHUNK, 3 * CHUNK), 1) % CHUNK
        m_fwd = (c <= r).astype(BF16)
        m_bwd = (c >= r).astype(BF16)
        lane = lax.broadcasted_iota(jnp.int32, (CHUNK, LANES), 1)
        for ci in range(tm // CHUNK):
            rows = slice(ci * CHUNK, (ci + 1) * CHUNK)
            g3 = jnp.concatenate([hi[rows], mid[rows], lo[rows]], axis=0)
            cum_f = _bdot(m_fwd, g3)
            cum_b = _bdot(m_bwd, g3)
            tile = jnp.where(lane < DN_HEADS, cum_f, jnp.where(lane < 2 * DN_HEADS, cum_b, beta[rows]))
            g_ref[rows, :] = tile
            gt_ref[ci] = tile.T[:N_GATE_COLS, :]

    def qkv_epilogue(c0, p):
        y = _silu(_conv3_rows(p, cwq_ref[:, c0:c0 + QKV_TILE], tm))
        if c0 >= 2 * DN_DIM:
            qkv_ref[:, c0:c0 + QKV_TILE] = y.astype(qkv_ref.dtype)
            return
        heads = []
        for hh in range(QKV_TILE // DN_HEAD_DIM):
            yh = y[:, hh * DN_HEAD_DIM:(hh + 1) * DN_HEAD_DIM]
            ss = jnp.sum(yh * yh, axis=-1, keepdims=True)
            sl = slice(c0 + hh * DN_HEAD_DIM, c0 + (hh + 1) * DN_HEAD_DIM)
            heads.append(yh * lax.rsqrt(ss + L2_EPS))
            qkv_ref[:, sl] = heads[-1].astype(qkv_ref.dtype)
        if c0 >= DN_DIM:
            pair0 = (c0 - DN_DIM) // (2 * DN_HEAD_DIM)
            for ci in range(tm // CHUNK):
                rows = slice(ci * CHUNK, (ci + 1) * CHUNK)
                for pp in range(len(heads) // 2):
                    both = jnp.concatenate([heads[2 * pp][rows], heads[2 * pp + 1][rows]], axis=0)
                    kt_ref[ci, pair0 + pp] = both.T.astype(kt_ref.dtype)

    def z_epilogue(c0, p):
        z_ref[:, c0:c0 + p.shape[1]] = _silu(p).astype(z_ref.dtype)

    def sc_epilogue(jn, p):
        for gi in range(SC_TILE_GROUPS):
            base = gi * 3 * gd
            b = p[HALO:HALO + tm, base:base + gd]
            cx = p[:, base + gd:base + 2 * gd] * p[:, base + 2 * gd:base + 3 * gd]
            g0 = (jn * SC_TILE_GROUPS + gi) * gd
            y = b * _conv3_rows(cx, cws_ref[:, g0:g0 + gd], tm)
            ysc_ref[:, g0:g0 + gd] = (_rms(y) * gain_ref[:, g0:g0 + gd]).astype(ysc_ref.dtype)

    part = functools.partial
    zq = QKV_TILE // 2

    def z_gate_epilogue(p):
        z_epilogue(DN_DIM - zq, p[:, :zq])
        gate_epilogue(p[:, zq:])

    tasks = [(part(plain_matmul, wz_ref, DN_DIM - zq, zq + LANES), z_gate_epilogue)]
    tasks += [(part(halo_matmul, wq_ref, c0, QKV_TILE), part(qkv_epilogue, c0))
              for c0 in range(0, 3 * DN_DIM, QKV_TILE)]
    tasks += [(part(plain_matmul, wz_ref, DN_DIM - 2 * zq, zq), part(z_epilogue, DN_DIM - 2 * zq))]
    tasks += [(part(halo_matmul, ws_ref, jn * sc_tile, sc_tile), part(sc_epilogue, jn))
              for jn in range(SC_GROUPS // SC_TILE_GROUPS)]
    tasks += [(part(plain_matmul, wz_ref, c0, QKV_TILE), part(z_epilogue, c0))
              for c0 in range(0, DN_DIM - 2 * zq, QKV_TILE)]
    _pipelined(tasks)


def _layer_resident(w, layer):
    return pl.BlockSpec((None,) + w.shape[1:], lambda *_: (layer, 0, 0), pipeline_mode=pl.Buffered(1))


def _in_proj(x, n_pre, w_qkv, w_zg, w_sc, layer, cw_qkv, cw_sc, gain, alog, dtb, tm):
    n, t, d = x.shape
    main, prev, nxt = _halo_specs(tm, t, d)
    whole = lambda shape: pl.BlockSpec(shape, lambda a, i: (0, 0))
    row = lambda width: pl.BlockSpec((None, tm, width), lambda a, i: (a, i, 0))
    return pl.pallas_call(
        functools.partial(_inproj_kernel, tm=tm),
        out_shape=(jax.ShapeDtypeStruct((n, t, 3 * DN_DIM), BF16),
                   jax.ShapeDtypeStruct((n, t, DN_DIM), BF16),
                   jax.ShapeDtypeStruct((n, t, LANES), F32),
                   jax.ShapeDtypeStruct((n, t, SC_DIM), BF16),
                   jax.ShapeDtypeStruct((n, t // CHUNK, DN_HEADS // 2, DN_HEAD_DIM, 2 * CHUNK), BF16),
                   jax.ShapeDtypeStruct((n, t // CHUNK, N_GATE_COLS, CHUNK), F32)),
        grid=(n, t // tm),
        in_specs=[main, prev, nxt, whole((1, d)),
                  _layer_resident(w_qkv, layer), _layer_resident(w_zg, layer), _layer_resident(w_sc, layer),
                  whole((3, 3 * DN_DIM)), whole((3, SC_DIM)), whole((1, SC_DIM)),
                  whole((1, LANES)), whole((1, LANES))],
        out_specs=(row(3 * DN_DIM), row(DN_DIM), row(LANES), row(SC_DIM),
                   pl.BlockSpec((None, tm // CHUNK, DN_HEADS // 2, DN_HEAD_DIM, 2 * CHUNK),
                                lambda a, i: (a, i, 0, 0, 0)),
                   pl.BlockSpec((None, tm // CHUNK, N_GATE_COLS, CHUNK), lambda a, i: (a, i, 0, 0))),
        compiler_params=_params(("parallel", "parallel")),
        name="in_proj",
    )(x, x, x, n_pre.reshape(1, d), w_qkv, w_zg, w_sc, cw_qkv, cw_sc, gain.reshape(1, SC_DIM), alog, dtb)


DELTA_UNROLL = 16
GROUP = 2


def _delta_kernel(qf_ref, kf_ref, vf_ref, gcf_ref, grf_ref, ktf_ref, qb_ref, kb_ref, vb_ref, gcb_ref, grb_ref,
                  ktb_ref, of_ref, ob_ref, s_ref, *, tb):
    @pl.when(pl.program_id(1) == 0)
    def _():
        s_ref[...] = jnp.zeros_like(s_ref)

    nch = tb // CHUNK
    unroll = math.gcd(nch, DELTA_UNROLL)
    pair_w = 2 * DN_HEAD_DIM
    r = lax.broadcasted_iota(jnp.int32, (CHUNK, 2 * CHUNK), 0)
    lane = lax.broadcasted_iota(jnp.int32, (CHUNK, 2 * CHUNK), 1)
    c = lane % CHUNK
    second = lane >= CHUNK
    eye = (r == c).astype(F32)
    masks = ((r >= c, r > c), (r <= c, r < c))
    row_head = lambda width: lax.broadcasted_iota(jnp.int32, (2 * CHUNK, width), 0) // CHUNK
    col_head = lambda width, per: lax.broadcasted_iota(jnp.int32, (2 * CHUNK, width), 1) // per
    diag_sq = (row_head(2 * CHUNK) == col_head(2 * CHUNK, CHUNK)).astype(BF16)
    diag_wide = (row_head(pair_w) == col_head(pair_w, DN_HEAD_DIM)).astype(BF16)
    zero_s = jnp.zeros((DN_HEAD_DIM, DN_HEAD_DIM), BF16)
    refs = ((qf_ref, kf_ref, vf_ref, gcf_ref, grf_ref, ktf_ref, of_ref),
            (qb_ref, kb_ref, vb_ref, gcb_ref, grb_ref, ktb_ref, ob_ref))

    def ld(ch, name):
        return ch[name + "_ref"][ch["rows"], ch["sl"]]

    def block_diag(x, sel):
        return jnp.concatenate([x, x], axis=0) * sel

    def pair_cols(x, h0):
        shape = (CHUNK, 2 * CHUNK)
        return jnp.where(second, jnp.broadcast_to(x[:, h0 + 1:h0 + 2], shape), jnp.broadcast_to(x[:, h0:h0 + 1], shape))

    def chains_of(jj):
        chains = []
        for d, (q_ref, k_ref, v_ref, gc_ref, gr_ref, kt_ref, o_ref) in enumerate(refs):
            cc = (nch - 1 - jj) if d else jj
            rows = pl.ds(pl.multiple_of(cc * CHUNK, CHUNK), CHUNK)
            gcol = gc_ref[rows, :]
            grow = gr_ref[cc]
            goff = d * DN_HEADS
            boff = 2 * DN_HEADS + goff
            cum_c = gcol[:, goff:goff + DN_HEADS]
            b_c = gcol[:, boff:boff + DN_HEADS]
            last = 0 if d else CHUNK - 1
            gl_c = cum_c[last:last + 1, :]
            eg_c = jnp.exp(cum_c) * Q_SCALE
            ekd_c = jnp.exp(gl_c - cum_c)
            egl = jnp.exp(gl_c)
            npair = DN_HEADS // 2
            for p in range(npair):
                h0 = 2 * p
                gr2 = grow[d * npair + p:d * npair + p + 1, :]
                br2 = grow[(2 + d) * npair + p:(2 + d) * npair + p + 1, :]
                chains.append(dict(
                    d=d, h0=h0, rows=rows, sl=slice(p * pair_w, (p + 1) * pair_w),
                    q_ref=q_ref, k_ref=k_ref, v_ref=v_ref, o_ref=o_ref, mle=masks[d][0], strict=masks[d][1],
                    kt_ref=kt_ref, cc=cc, p=p,
                    gc=pair_cols(cum_c, h0), bc=pair_cols(b_c, h0), gr=gr2, br=br2, ebr=br2 * jnp.exp(gr2),
                    egc=[eg_c[:, h0 + j:h0 + j + 1] for j in range(2)],
                    ekd=[ekd_c[:, h0 + j:h0 + j + 1] for j in range(2)],
                    egl=[egl[:, h0 + j:h0 + j + 1] for j in range(2)]))
        return chains

    def head(x, j):
        return x[:, j * DN_HEAD_DIM:(j + 1) * DN_HEAD_DIM]

    def bd_sq(x):
        return block_diag(x.astype(BF16), diag_sq)

    def state_free(chains, out):
        decay = [jnp.where(ch["mle"], jnp.exp(jnp.where(ch["mle"], ch["gc"] - ch["gr"], 0.0)), 0.0)
                 for ch in chains]
        bdk = [block_diag(ld(ch, "k"), diag_wide) for ch in chains]
        kq = [lax.dot_general(jnp.concatenate([ld(ch, "k"), ld(ch, "q")], axis=0), x, NT_DIMS,
                              preferred_element_type=F32) for ch, x in zip(chains, bdk)]
        yield
        a_qk = [(x[CHUNK:] * dc * Q_SCALE).astype(BF16) for x, dc in zip(kq, decay)]
        pw = [jnp.where(ch["strict"], -(x[:CHUNK] * dc) * ch["bc"], 0.0) for ch, x, dc in zip(chains, kq, decay)]
        tinv = [eye + p for p in pw]
        pw = [_bdot(p.astype(BF16), bd_sq(p)) for p in pw]
        yield
        for _ in range(4):
            both = [_bdot(jnp.concatenate([t, p], axis=0).astype(BF16), bd_sq(p)) for t, p in zip(tinv, pw)]
            tinv = [t + b[:CHUNK] for t, b in zip(tinv, both)]
            pw = [b[CHUNK:] for b in both]
            yield
        tinv = [t + _bdot(t.astype(BF16), bd_sq(p)) for t, p in zip(tinv, pw)]
        yield
        bdv = [block_diag(ld(ch, "v"), diag_wide) for ch in chains]
        u = [_bdot((t * ch["br"]).astype(BF16), x) for ch, t, x in zip(chains, tinv, bdv)]
        w = [_bdot((t * ch["ebr"]).astype(BF16), x) for ch, t, x in zip(chains, tinv, bdk)]
        qg = [jnp.concatenate([(head(ld(ch, "q"), j).astype(F32) * ch["egc"][j]).astype(BF16) for j in range(2)],
                              axis=1) for ch in chains]
        out.extend(zip(a_qk, u, w, qg))

    def state_step(chains, free, s):
        s_bd = [jnp.concatenate([jnp.concatenate([z[0].astype(BF16), zero_s], axis=1),
                                 jnp.concatenate([zero_s, z[1].astype(BF16)], axis=1)], axis=0) for z in s]
        wq = [_bdot(jnp.concatenate([w.astype(BF16), qg], axis=0), z)
              for (_, _, w, qg), z in zip(free, s_bd)]
        yield
        vn = [u - y[:CHUNK] for (_, u, _, _), y in zip(free, wq)]
        for ch, y, (a_qk, _, _, _), x in zip(chains, wq, free, vn):
            o = y[CHUNK:] + _bdot(a_qk, block_diag(x.astype(BF16), diag_wide))
            ch["o_ref"][ch["rows"], ch["sl"]] = o.astype(ch["o_ref"].dtype)
        vd = [jnp.concatenate([(head(x, j) * ch["ekd"][j]).astype(BF16) for j in range(2)], axis=1)
              for ch, x in zip(chains, vn)]
        upd = [_bdot(ch["kt_ref"][ch["cc"], ch["p"]], block_diag(x, diag_wide)) for ch, x in zip(chains, vd)]
        yield
        s[:] = [[z[j] * ch["egl"][j] + head(y, j) for j in range(2)] for ch, z, y in zip(chains, s, upd)]

    def alternate(first, second):
        for _ in first:
            next(second, None)
        for _ in second:
            pass

    def body(it, carry):
        chunks = [chains_of(it * unroll + j) for j in range(unroll)]
        s = [[s_ref[ch["d"], ch["h0"] + j] for j in range(2)] for ch in chunks[0]]
        free = []

        def free_of(lo, hi):
            return state_free([ch for chains in chunks[lo:hi] for ch in chains], free)

        def steps(lo, hi):
            for j in range(lo, hi):
                yield from state_step(chunks[j], free[j * len(chunks[j]):(j + 1) * len(chunks[j])], s)

        bounds = list(range(0, unroll, GROUP)) + [unroll]
        alternate(free_of(bounds[0], bounds[1]), iter(()))
        for g in range(1, len(bounds) - 1):
            alternate(free_of(bounds[g], bounds[g + 1]), steps(bounds[g - 1], bounds[g]))
        alternate(steps(bounds[-2], bounds[-1]), iter(()))
        for ch, z in zip(chunks[0], s):
            for j in range(2):
                s_ref[ch["d"], ch["h0"] + j] = z[j]
        return carry

    lax.fori_loop(0, nch // unroll, body, 0)


def _delta_scan(qkv, gates, gates_t, k_t, tb):
    n, t, _ = qkv.shape
    nb = t // tb
    nch = tb // CHUNK
    fwd = lambda i: i
    bwd = lambda i: nb - 1 - i

    def specs(blk):
        return [pl.BlockSpec((None, tb, DN_DIM), lambda a, i: (a, blk(i), 0)),
                pl.BlockSpec((None, tb, DN_DIM), lambda a, i: (a, blk(i), 1)),
                pl.BlockSpec((None, tb, DN_DIM), lambda a, i: (a, blk(i), 2)),
                pl.BlockSpec((None, tb, LANES), lambda a, i: (a, blk(i), 0)),
                pl.BlockSpec((None, nch, N_GATE_COLS // 2, 2 * CHUNK), lambda a, i: (a, blk(i), 0, 0)),
                pl.BlockSpec((None, nch, DN_HEADS // 2, DN_HEAD_DIM, 2 * CHUNK), lambda a, i: (a, blk(i), 0, 0, 0))]

    out = jax.ShapeDtypeStruct((n, t, DN_DIM), BF16)
    args = (qkv, qkv, qkv, gates, gates_t, k_t)
    return pl.pallas_call(
        functools.partial(_delta_kernel, tb=tb),
        out_shape=(out, out),
        grid=(n, nb),
        in_specs=specs(fwd) + specs(bwd),
        out_specs=(pl.BlockSpec((None, tb, DN_DIM), lambda a, i: (a, fwd(i), 0)),
                   pl.BlockSpec((None, tb, DN_DIM), lambda a, i: (a, bwd(i), 0))),
        scratch_shapes=[pltpu.VMEM((2, DN_HEADS, DN_HEAD_DIM, DN_HEAD_DIM), F32)],
        compiler_params=_params(("parallel", "arbitrary")),
        name="delta_scan",
    )(*args, *args)


OUT_ROWS = 256


def _out_kernel(of_ref, ob_ref, zs_ref, ysc_ref, x_ref, w_ref, dnw_ref, npost_ref, nnext_ref,
                xo_ref, hn_ref, odn_ref, *, tm):
    def matmul(r0):
        rows = slice(r0, r0 + OUT_ROWS)
        for hh in range(DN_HEADS):
            sl = slice(hh * DN_HEAD_DIM, (hh + 1) * DN_HEAD_DIM)
            o = of_ref[rows, sl].astype(F32) + ob_ref[rows, sl].astype(F32)
            y = _rms(o) * dnw_ref[...] * zs_ref[rows, sl].astype(F32)
            odn_ref[rows, sl] = y.astype(odn_ref.dtype)
        return _bdot(odn_ref[rows, :], w_ref[0:DN_DIM, :]) + _bdot(ysc_ref[rows, :], w_ref[DN_DIM:, :])

    def epilogue(r0, m):
        rows = slice(r0, r0 + OUT_ROWS)
        xn = x_ref[rows, :] + _rms(m) * npost_ref[...]
        xo_ref[rows, :] = xn
        hn_ref[rows, :] = (_rms(xn) * nnext_ref[...]).astype(hn_ref.dtype)

    _pipelined([(functools.partial(matmul, r0), functools.partial(epilogue, r0))
                for r0 in range(0, tm, OUT_ROWS)])


def _out_proj(o_f, o_b, zs, ysc, x, w, layer, dn_w, n_post, n_next, tm):
    n, t, d = x.shape
    assert tm % OUT_ROWS == 0
    row = lambda width: pl.BlockSpec((None, tm, width), lambda a, i: (a, i, 0))
    vec = lambda width: pl.BlockSpec((1, width), lambda a, i: (0, 0))
    return pl.pallas_call(
        functools.partial(_out_kernel, tm=tm),
        out_shape=(jax.ShapeDtypeStruct((n, t, d), F32), jax.ShapeDtypeStruct((n, t, d), BF16)),
        grid=(n, t // tm),
        in_specs=[row(DN_DIM), row(DN_DIM), row(DN_DIM), row(SC_DIM), row(d),
                  _layer_resident(w, layer),
                  vec(DN_HEAD_DIM), vec(d), vec(d)],
        out_specs=(row(d), row(d)),
        scratch_shapes=[pltpu.VMEM((tm, DN_DIM), BF16)],
        compiler_params=_params(("parallel", "parallel")),
        name="out_proj",
    )(o_f, o_b, zs, ysc, x, w, dn_w.reshape(1, -1), n_post.reshape(1, d), n_next.reshape(1, d))


def _ffn_kernel(h_ref, hp_ref, hn_ref, wa_ref, wb_ref, cw_ref, wd_ref, x_ref, npost_ref,
                xo_ref, hx_ref, acc_ref, *, tm):
    i = pl.program_id(1)
    f = pl.program_id(2)

    @pl.when(f == 0)
    def _():
        _fill_halo(hx_ref, h_ref, hp_ref, hn_ref, tm, i == 0, i == pl.num_programs(1) - 1)
        acc_ref[...] = jnp.zeros_like(acc_ref)

    a = _bdot(hx_ref[...], wa_ref[...])
    b = _bdot(h_ref[...], wb_ref[...])
    act = (_silu(_conv3_rows(a, cw_ref[...], tm)) * b).astype(BF16)
    acc_ref[...] += _bdot(act, wd_ref[...])

    @pl.when(f == pl.num_programs(2) - 1)
    def _():
        xo_ref[...] = x_ref[...] + _rms(acc_ref[...]) * npost_ref[...]


def _ffn(h, x, w_up, cw, w_down, layer, n_post, tm, tf=512):
    n, t, d = x.shape
    nf = D_FF // tf
    main, prev, nxt = _halo_specs(tm, t, d)
    row = lambda: pl.BlockSpec((None, tm, d), lambda a, i, f: (a, i, 0))
    vec = lambda: pl.BlockSpec((1, d), lambda a, i, f: (0, 0))
    return pl.pallas_call(
        functools.partial(_ffn_kernel, tm=tm),
        out_shape=jax.ShapeDtypeStruct((n, t, d), F32),
        grid=(n, t // tm, nf),
        in_specs=[main, prev, nxt,
                  pl.BlockSpec((None, d, tf), lambda a, i, f: (layer, 0, f)),
                  pl.BlockSpec((None, d, tf), lambda a, i, f: (layer, 0, f + nf)),
                  pl.BlockSpec((3, tf), lambda a, i, f: (0, f)),
                  pl.BlockSpec((None, tf, d), lambda a, i, f: (layer, f, 0)),
                  row(), vec()],
        out_specs=row(),
        scratch_shapes=[pltpu.VMEM((tm + 2 * HALO, d), BF16), pltpu.VMEM((tm, d), F32)],
        compiler_params=_params(("parallel", "parallel", "arbitrary")),
        name="ffn",
    )(h, h, h, w_up, w_up, cw, w_down, x, n_post.reshape(1, d))


def _tiles(t):
    return min(512, t), min(DELTA_UNROLL * CHUNK, t)


def _win_kernel(w_ref, q_ref, zg_ref, sc_ref):
    z_end = 4 * DN_DIM
    g_end = z_end + N_GATE_COLS
    q_ref[...] = w_ref[:, :3 * DN_DIM].astype(q_ref.dtype)
    zg_ref[:, :DN_DIM] = w_ref[:, 3 * DN_DIM:z_end].astype(zg_ref.dtype)
    gate = w_ref[:, z_end:z_end + LANES]
    lane = lax.broadcasted_iota(jnp.int32, gate.shape, 1)
    zg_ref[:, DN_DIM:] = jnp.where(lane < N_GATE_COLS, gate, 0.0).astype(zg_ref.dtype)
    gd = SC_GROUP_DIM
    for g in range(SC_GROUPS):
        for seg in range(3):
            src = g_end + seg * SC_DIM + g * gd
            sc_ref[:, (3 * g + seg) * gd:(3 * g + seg + 1) * gd] = w_ref[:, src:src + gd].astype(sc_ref.dtype)


def _prepare_w_in(w_in, tk=256):
    depth, d, cols = w_in.shape
    out = lambda width: jax.ShapeDtypeStruct((depth, d, width), BF16)
    spec = lambda width: pl.BlockSpec((None, tk, width), lambda l, i: (l, i, 0))
    return pl.pallas_call(
        _win_kernel,
        out_shape=(out(3 * DN_DIM), out(DN_DIM + LANES), out(3 * SC_DIM)),
        grid=(depth, d // tk),
        in_specs=[spec(cols)],
        out_specs=(spec(3 * DN_DIM), spec(DN_DIM + LANES), spec(3 * SC_DIM)),
        compiler_params=_params(("parallel", "parallel")),
        name="w_in_prep",
    )(w_in)


def _prepare(w_in, a_log, dt_bias):
    depth = w_in.shape[0]
    w_qkv, w_zg, w_sc = _prepare_w_in(w_in)
    vpad = jnp.zeros((depth, 1, LANES - 2 * DN_HEADS), F32)
    alog = jnp.concatenate([a_log.reshape(depth, 1, 2 * DN_HEADS).astype(F32), vpad], axis=-1)
    dtb = jnp.concatenate([dt_bias.reshape(depth, 1, 2 * DN_HEADS).astype(F32), vpad], axis=-1)
    return w_qkv, w_zg, w_sc, alog, dtb


def _trunk(x, prm):
    (norm_mix_pre, w_qkv, w_zg, w_sc, alog, dtb, conv_qkv, dn_norm, conv_sc, sc_norm, w_out,
     norm_mix_post, norm_ffn_pre, w_up, conv_ffn, w_down, norm_ffn_post) = prm
    depth = w_qkv.shape[0]
    n, t, _ = x.shape
    tm, tb = _tiles(t)
    for l in range(depth):
        qkv, zs, gates, ysc, k_t, gates_t = _in_proj(x, norm_mix_pre[l], w_qkv, w_zg, w_sc, l,
                                                     conv_qkv[l], conv_sc[l], sc_norm[l], alog[l], dtb[l], tm)
        gates_t = gates_t.reshape(n, t // CHUNK, N_GATE_COLS // 2, 2 * CHUNK)
        o_f, o_b = _delta_scan(qkv, gates, gates_t, k_t, tb)
        x, h = _out_proj(o_f, o_b, zs, ysc, x, w_out, l, dn_norm[l], norm_mix_post[l], norm_ffn_pre[l], tm)
        x = _ffn(h, x, w_up, conv_ffn[l], w_down, l, norm_ffn_post[l], tm)
    return x


def kernel(x_prompt, x_sample, norm_mix_pre, w_in, conv_qkv, a_log, dt_bias, dn_norm, conv_sc, sc_norm, w_out, norm_mix_post, norm_ffn_pre, w_up, conv_ffn, w_down, norm_ffn_post):
    w_qkv, w_zg, w_sc, alog, dtb = _prepare(w_in, a_log, dt_bias)
    prm = (norm_mix_pre, w_qkv, w_zg, w_sc, alog, dtb, conv_qkv, dn_norm, conv_sc, sc_norm,
           w_out.astype(BF16), norm_mix_post, norm_ffn_pre, w_up.astype(BF16), conv_ffn,
           w_down.astype(BF16), norm_ffn_post)
    return (_trunk(x_prompt, prm), _trunk(x_sample, prm))
```

```python
import functools
import math

import jax
import jax.numpy as jnp
from jax import lax
from jax.experimental import pallas as pl
from jax.experimental.pallas import tpu as pltpu

D_MODEL = 2048
DN_HEADS = 8
DN_HEAD_DIM = 128
DN_DIM = DN_HEADS * DN_HEAD_DIM
SC_GROUPS = 8
SC_GROUP_DIM = 128
SC_DIM = SC_GROUPS * SC_GROUP_DIM
D_FF = 5632
CHUNK = 64
N_GATE_COLS = 4 * DN_HEADS
NORM_EPS = 1e-6
L2_EPS = 1e-6
Q_SCALE = DN_HEAD_DIM ** -0.5

LANES = 128
HALO = 16
VMEM_LIMIT = 56 * 1024 * 1024

F32 = jnp.float32
BF16 = jnp.bfloat16
NT_DIMS = (((1,), (1,)), ((), ()))
TN_DIMS = (((0,), (0,)), ((), ()))


def _sigmoid(x):
    return 1.0 / (1.0 + jnp.exp(-x))


def _silu(x):
    return x * _sigmoid(x)


def _softplus(x):
    return jnp.maximum(x, 0.0) + jnp.log1p(jnp.exp(-jnp.abs(x)))


def _rms(x):
    return x * lax.rsqrt(jnp.mean(x * x, axis=-1, keepdims=True) + NORM_EPS)


def _bdot(a, b):
    return jnp.dot(a, b, preferred_element_type=F32)


def _conv3_rows(p, cw, tm):
    rows = p.shape[0]
    prev = pltpu.roll(p, 1, 0)[HALO:HALO + tm]
    nxt = pltpu.roll(p, rows - 1, 0)[HALO:HALO + tm]
    mid = p[HALO:HALO + tm]
    return prev * cw[0:1] + mid * cw[1:2] + nxt * cw[2:3]


def _with_halo(h, hp, hn, first, last):
    return jnp.concatenate([jnp.where(first, jnp.zeros_like(hp), hp), h,
                            jnp.where(last, jnp.zeros_like(hn), hn)], axis=0)


def _halo_specs(tm, t_len, width):
    per = tm // HALO
    nblk = t_len // HALO
    main = pl.BlockSpec((None, tm, width), lambda n, i, *_: (n, i, 0))
    prev = pl.BlockSpec((None, HALO, width), lambda n, i, *_: (n, jnp.maximum(i * per - 1, 0), 0))
    nxt = pl.BlockSpec((None, HALO, width), lambda n, i, *_: (n, jnp.minimum((i + 1) * per, nblk - 1), 0))
    return main, prev, nxt


def _params(sem):
    return pltpu.CompilerParams(dimension_semantics=sem, vmem_limit_bytes=VMEM_LIMIT)


QKV_TILE = 512
SC_TILE_GROUPS = 2


def _pipelined(tasks):
    pending = None
    for matmul, epilogue in tasks:
        val = matmul()
        if pending is not None:
            pending[0](pending[1])
        pending = (epilogue, val)
    pending[0](pending[1])


def _split3(x):
    hi = x.astype(BF16)
    r1 = x - hi.astype(F32)
    mid = r1.astype(BF16)
    lo = (r1 - mid.astype(F32)).astype(BF16)
    return hi, mid, lo


def _inproj_kernel(h_ref, hp_ref, hn_ref, nw_ref, wq_ref, wz_ref, ws_ref, cwq_ref, cws_ref, gain_ref, alog_ref,
                   dtb_ref, qkv_ref, z_ref, g_ref, ysc_ref, kt_ref, gt_ref, *, tm):
    i = pl.program_id(1)

    def rows_of(ref):
        return (_rms(ref[...]) * nw_ref[...]).astype(BF16)

    h = rows_of(h_ref)
    hx = _with_halo(h, rows_of(hp_ref), rows_of(hn_ref), i == 0, i == pl.num_programs(1) - 1)
    gd = SC_GROUP_DIM
    sc_tile = SC_TILE_GROUPS * 3 * gd

    def halo_matmul(w_ref, c0, width):
        return _bdot(hx, w_ref[:, c0:c0 + width])

    def plain_matmul(w_ref, c0, width):
        return _bdot(h, w_ref[:, c0:c0 + width])

    def gate_epilogue(gl):
        g = -jnp.exp(alog_ref[...]) * _softplus(gl + dtb_ref[...])
        beta = _sigmoid(gl)
        hi, mid, lo = _split3(g)
        r = lax.broadcasted_iota(jnp.int32, (CHUNK, 3 * CHUNK), 0)
        c = lax.broadcasted_iota(jnp.int32, (CHUNK, 3 * CHUNK), 1) % CHUNK
        m_fwd = (c <= r).astype(BF16)
        m_bwd = (c >= r).astype(BF16)
        lane = lax.broadcasted_iota(jnp.int32, (CHUNK, LANES), 1)
        for ci in range(tm // CHUNK):
            rows = slice(ci * CHUNK, (ci + 1) * CHUNK)
            g3 = jnp.concatenate([hi[rows], mid[rows], lo[rows]], axis=0)
            cum_f = _bdot(m_fwd, g3)
            cum_b = _bdot(m_bwd, g3)
            tile = jnp.where(lane < DN_HEADS, cum_f, jnp.where(lane < 2 * DN_HEADS, cum_b, beta[rows]))
            g_ref[rows, :] = tile
            gt_ref[ci] = tile.T[:N_GATE_COLS, :]

    def qkv_epilogue(c0, p):
        y = _silu(_conv3_rows(p, cwq_ref[:, c0:c0 + QKV_TILE], tm))
        if c0 >= 2 * DN_DIM:
            qkv_ref[:, c0:c0 + QKV_TILE] = y.astype(qkv_ref.dtype)
            return
        heads = []
        for hh in range(QKV_TILE // DN_HEAD_DIM):
            yh = y[:, hh * DN_HEAD_DIM:(hh + 1) * DN_HEAD_DIM]
            ss = jnp.sum(yh * yh, axis=-1, keepdims=True)
            sl = slice(c0 + hh * DN_HEAD_DIM, c0 + (hh + 1) * DN_HEAD_DIM)
            heads.append(yh * lax.rsqrt(ss + L2_EPS))
            qkv_ref[:, sl] = heads[-1].astype(qkv_ref.dtype)
        if c0 >= DN_DIM:
            pair0 = (c0 - DN_DIM) // (2 * DN_HEAD_DIM)
            for ci in range(tm // CHUNK):
                rows = slice(ci * CHUNK, (ci + 1) * CHUNK)
                for pp in range(len(heads) // 2):
                    both = jnp.concatenate([heads[2 * pp][rows], heads[2 * pp + 1][rows]], axis=0)
                    kt_ref[ci, pair0 + pp] = both.T.astype(kt_ref.dtype)

    def z_epilogue(c0, p):
        z_ref[:, c0:c0 + p.shape[1]] = _silu(p).astype(z_ref.dtype)

    def sc_epilogue(jn, p):
        for gi in range(SC_TILE_GROUPS):
            base = gi * 3 * gd
            b = p[HALO:HALO + tm, base:base + gd]
            cx = p[:, base + gd:base + 2 * gd] * p[:, base + 2 * gd:base + 3 * gd]
            g0 = (jn * SC_TILE_GROUPS + gi) * gd
            y = b * _conv3_rows(cx, cws_ref[:, g0:g0 + gd], tm)
            ysc_ref[:, g0:g0 + gd] = (_rms(y) * gain_ref[:, g0:g0 + gd]).astype(ysc_ref.dtype)

    part = functools.partial
    zq = QKV_TILE // 2

    def z_gate_epilogue(p):
        z_epilogue(DN_DIM - zq, p[:, :zq])
        gate_epilogue(p[:, zq:])

    tasks = [(part(plain_matmul, wz_ref, DN_DIM - zq, zq + LANES), z_gate_epilogue)]
    tasks += [(part(halo_matmul, wq_ref, c0, QKV_TILE), part(qkv_epilogue, c0))
              for c0 in range(0, 3 * DN_DIM, QKV_TILE)]
    tasks += [(part(plain_matmul, wz_ref, DN_DIM - 2 * zq, zq), part(z_epilogue, DN_DIM - 2 * zq))]
    tasks += [(part(halo_matmul, ws_ref, jn * sc_tile, sc_tile), part(sc_epilogue, jn))
              for jn in range(SC_GROUPS // SC_TILE_GROUPS)]
    tasks += [(part(plain_matmul, wz_ref, c0, QKV_TILE), part(z_epilogue, c0))
              for c0 in range(0, DN_DIM - 2 * zq, QKV_TILE)]
    _pipelined(tasks)


def _layer_resident(w, layer):
    return pl.BlockSpec((None,) + w.shape[1:], lambda *_: (layer, 0, 0), pipeline_mode=pl.Buffered(1))


def _in_proj(x, n_pre, w_qkv, w_zg, w_sc, layer, cw_qkv, cw_sc, gain, alog, dtb, tm):
    n, t, d = x.shape
    main, prev, nxt = _halo_specs(tm, t, d)
    whole = lambda shape: pl.BlockSpec(shape, lambda a, i: (0, 0))
    row = lambda width: pl.BlockSpec((None, tm, width), lambda a, i: (a, i, 0))
    return pl.pallas_call(
        functools.partial(_inproj_kernel, tm=tm),
        out_shape=(jax.ShapeDtypeStruct((n, t, 3 * DN_DIM), BF16),
                   jax.ShapeDtypeStruct((n, t, DN_DIM), BF16),
                   jax.ShapeDtypeStruct((n, t, LANES), F32),
                   jax.ShapeDtypeStruct((n, t, SC_DIM), BF16),
                   jax.ShapeDtypeStruct((n, t // CHUNK, DN_HEADS // 2, DN_HEAD_DIM, 2 * CHUNK), BF16),
                   jax.ShapeDtypeStruct((n, t // CHUNK, N_GATE_COLS, CHUNK), F32)),
        grid=(n, t // tm),
        in_specs=[main, prev, nxt, whole((1, d)),
                  _layer_resident(w_qkv, layer), _layer_resident(w_zg, layer), _layer_resident(w_sc, layer),
                  whole((3, 3 * DN_DIM)), whole((3, SC_DIM)), whole((1, SC_DIM)),
                  whole((1, LANES)), whole((1, LANES))],
        out_specs=(row(3 * DN_DIM), row(DN_DIM), row(LANES), row(SC_DIM),
                   pl.BlockSpec((None, tm // CHUNK, DN_HEADS // 2, DN_HEAD_DIM, 2 * CHUNK),
                                lambda a, i: (a, i, 0, 0, 0)),
                   pl.BlockSpec((None, tm // CHUNK, N_GATE_COLS, CHUNK), lambda a, i: (a, i, 0, 0))),
        compiler_params=_params(("parallel", "parallel")),
        name="in_proj",
    )(x, x, x, n_pre.reshape(1, d), w_qkv, w_zg, w_sc, cw_qkv, cw_sc, gain.reshape(1, SC_DIM), alog, dtb)


DELTA_UNROLL = 16
GROUP = 2


def _delta_kernel(qf_ref, kf_ref, vf_ref, gcf_ref, grf_ref, ktf_ref, qb_ref, kb_ref, vb_ref, gcb_ref, grb_ref,
                  ktb_ref, of_ref, ob_ref, s_ref, *, tb):
    @pl.when(pl.program_id(1) == 0)
    def _():
        s_ref[...] = jnp.zeros_like(s_ref)

    nch = tb // CHUNK
    unroll = math.gcd(nch, DELTA_UNROLL)
    pair_w = 2 * DN_HEAD_DIM
    r = lax.broadcasted_iota(jnp.int32, (CHUNK, 2 * CHUNK), 0)
    lane = lax.broadcasted_iota(jnp.int32, (CHUNK, 2 * CHUNK), 1)
    c = lane % CHUNK
    second = lane >= CHUNK
    eye = (r == c).astype(F32)
    masks = ((r >= c, r > c), (r <= c, r < c))
    row_head = lambda width: lax.broadcasted_iota(jnp.int32, (2 * CHUNK, width), 0) // CHUNK
    col_head = lambda width, per: lax.broadcasted_iota(jnp.int32, (2 * CHUNK, width), 1) // per
    diag_sq = (row_head(2 * CHUNK) == col_head(2 * CHUNK, CHUNK)).astype(BF16)
    diag_wide = (row_head(pair_w) == col_head(pair_w, DN_HEAD_DIM)).astype(BF16)
    zero_s = jnp.zeros((DN_HEAD_DIM, DN_HEAD_DIM), BF16)
    refs = ((qf_ref, kf_ref, vf_ref, gcf_ref, grf_ref, ktf_ref, of_ref),
            (qb_ref, kb_ref, vb_ref, gcb_ref, grb_ref, ktb_ref, ob_ref))

    def ld(ch, name):
        return ch[name + "_ref"][ch["rows"], ch["sl"]]

    def block_diag(x, sel):
        return jnp.concatenate([x, x], axis=0) * sel

    def pair_cols(x, h0):
        shape = (CHUNK, 2 * CHUNK)
        return jnp.where(second, jnp.broadcast_to(x[:, h0 + 1:h0 + 2], shape), jnp.broadcast_to(x[:, h0:h0 + 1], shape))

    def chains_of(jj):
        chains = []
        for d, (q_ref, k_ref, v_ref, gc_ref, gr_ref, kt_ref, o_ref) in enumerate(refs):
            cc = (nch - 1 - jj) if d else jj
            rows = pl.ds(pl.multiple_of(cc * CHUNK, CHUNK), CHUNK)
            gcol = gc_ref[rows, :]
            grow = gr_ref[cc]
            goff = d * DN_HEADS
            boff = 2 * DN_HEADS + goff
            cum_c = gcol[:, goff:goff + DN_HEADS]
            b_c = gcol[:, boff:boff + DN_HEADS]
            last = 0 if d else CHUNK - 1
            gl_c = cum_c[last:last + 1, :]
            eg_c = jnp.exp(cum_c) * Q_SCALE
            ekd_c = jnp.exp(gl_c - cum_c)
            egl = jnp.exp(gl_c)
            npair = DN_HEADS // 2
            for p in range(npair):
                h0 = 2 * p
                gr2 = grow[d * npair + p:d * npair + p + 1, :]
                br2 = grow[(2 + d) * npair + p:(2 + d) * npair + p + 1, :]
                chains.append(dict(
                    d=d, h0=h0, rows=rows, sl=slice(p * pair_w, (p + 1) * pair_w),
                    q_ref=q_ref, k_ref=k_ref, v_ref=v_ref, o_ref=o_ref, mle=masks[d][0], strict=masks[d][1],
                    kt_ref=kt_ref, cc=cc, p=p,
                    gc=pair_cols(cum_c, h0), bc=pair_cols(b_c, h0), gr=gr2, br=br2, ebr=br2 * jnp.exp(gr2),
                    egc=[eg_c[:, h0 + j:h0 + j + 1] for j in range(2)],
                    ekd=[ekd_c[:, h0 + j:h0 + j + 1] for j in range(2)],
                    egl=[egl[:, h0 + j:h0 + j + 1] for j in range(2)]))
        return chains

    def head(x, j):
        return x[:, j * DN_HEAD_DIM:(j + 1) * DN_HEAD_DIM]

    def bd_sq(x):
        return block_diag(x.astype(BF16), diag_sq)

    def state_free(chains, out):
        decay = [jnp.where(ch["mle"], jnp.exp(jnp.where(ch["mle"], ch["gc"] - ch["gr"], 0.0)), 0.0)
                 for ch in chains]
        bdk = [block_diag(ld(ch, "k"), diag_wide) for ch in chains]
        kq = [lax.dot_general(jnp.concatenate([ld(ch, "k"), ld(ch, "q")], axis=0), x, NT_DIMS,
                              preferred_element_type=F32) for ch, x in zip(chains, bdk)]
        yield
        a_qk = [(x[CHUNK:] * dc * Q_SCALE).astype(BF16) for x, dc in zip(kq, decay)]
        pw = [jnp.where(ch["strict"], -(x[:CHUNK] * dc) * ch["bc"], 0.0) for ch, x, dc in zip(chains, kq, decay)]
        tinv = [eye + p for p in pw]
        pw = [_bdot(p.astype(BF16), bd_sq(p)) for p in pw]
        yield
        for _ in range(4):
            both = [_bdot(jnp.concatenate([t, p], axis=0).astype(BF16), bd_sq(p)) for t, p in zip(tinv, pw)]
            tinv = [t + b[:CHUNK] for t, b in zip(tinv, both)]
            pw = [b[CHUNK:] for b in both]
            yield
        tinv = [t + _bdot(t.astype(BF16), bd_sq(p)) for t, p in zip(tinv, pw)]
        yield
        bdv = [block_diag(ld(ch, "v"), diag_wide) for ch in chains]
        u = [_bdot((t * ch["br"]).astype(BF16), x) for ch, t, x in zip(chains, tinv, bdv)]
        w = [_bdot((t * ch["ebr"]).astype(BF16), x) for ch, t, x in zip(chains, tinv, bdk)]
        qg = [jnp.concatenate([(head(ld(ch, "q"), j).astype(F32) * ch["egc"][j]).astype(BF16) for j in range(2)],
                              axis=1) for ch in chains]
        out.extend(zip(a_qk, u, w, qg))

    def state_step(chains, free, s):
        s_bd = [jnp.concatenate([jnp.concatenate([z[0].astype(BF16), zero_s], axis=1),
                                 jnp.concatenate([zero_s, z[1].astype(BF16)], axis=1)], axis=0) for z in s]
        wq = [_bdot(jnp.concatenate([w.astype(BF16), qg], axis=0), z)
              for (_, _, w, qg), z in zip(free, s_bd)]
        yield
        vn = [u - y[:CHUNK] for (_, u, _, _), y in zip(free, wq)]
        for ch, y, (a_qk, _, _, _), x in zip(chains, wq, free, vn):
            o = y[CHUNK:] + _bdot(a_qk, block_diag(x.astype(BF16), diag_wide))
            ch["o_ref"][ch["rows"], ch["sl"]] = o.astype(ch["o_ref"].dtype)
        vd = [jnp.concatenate([(head(x, j) * ch["ekd"][j]).astype(BF16) for j in range(2)], axis=1)
              for ch, x in zip(chains, vn)]
        upd = [_bdot(ch["kt_ref"][ch["cc"], ch["p"]], block_diag(x, diag_wide)) for ch, x in zip(chains, vd)]
        yield
        s[:] = [[z[j] * ch["egl"][j] + head(y, j) for j in range(2)] for ch, z, y in zip(chains, s, upd)]

    def alternate(first, second):
        for _ in first:
            next(second, None)
        for _ in second:
            pass

    def body(it, carry):
        chunks = [chains_of(it * unroll + j) for j in range(unroll)]
        s = [[s_ref[ch["d"], ch["h0"] + j] for j in range(2)] for ch in chunks[0]]
        free = []

        def free_of(lo, hi):
            return state_free([ch for chains in chunks[lo:hi] for ch in chains], free)

        def steps(lo, hi):
            for j in range(lo, hi):
                yield from state_step(chunks[j], free[j * len(chunks[j]):(j + 1) * len(chunks[j])], s)

        bounds = list(range(0, unroll, GROUP)) + [unroll]
        alternate(free_of(bounds[0], bounds[1]), iter(()))
        for g in range(1, len(bounds) - 1):
            alternate(free_of(bounds[g], bounds[g + 1]), steps(bounds[g - 1], bounds[g]))
        alternate(steps(bounds[-2], bounds[-1]), iter(()))
        for ch, z in zip(chunks[0], s):
            for j in range(2):
                s_ref[ch["d"], ch["h0"] + j] = z[j]
        return carry

    lax.fori_loop(0, nch // unroll, body, 0)


def _delta_scan(qkv, gates, gates_t, k_t, tb):
    n, t, _ = qkv.shape
    nb = t // tb
    nch = tb // CHUNK
    fwd = lambda i: i
    bwd = lambda i: nb - 1 - i

    def specs(blk):
        return [pl.BlockSpec((None, tb, DN_DIM), lambda a, i: (a, blk(i), 0)),
                pl.BlockSpec((None, tb, DN_DIM), lambda a, i: (a, blk(i), 1)),
                pl.BlockSpec((None, tb, DN_DIM), lambda a, i: (a, blk(i), 2)),
                pl.BlockSpec((None, tb, LANES), lambda a, i: (a, blk(i), 0)),
                pl.BlockSpec((None, nch, N_GATE_COLS // 2, 2 * CHUNK), lambda a, i: (a, blk(i), 0, 0)),
                pl.BlockSpec((None, nch, DN_HEADS // 2, DN_HEAD_DIM, 2 * CHUNK), lambda a, i: (a, blk(i), 0, 0, 0))]

    out = jax.ShapeDtypeStruct((n, t, DN_DIM), BF16)
    args = (qkv, qkv, qkv, gates, gates_t, k_t)
    return pl.pallas_call(
        functools.partial(_delta_kernel, tb=tb),
        out_shape=(out, out),
        grid=(n, nb),
        in_specs=specs(fwd) + specs(bwd),
        out_specs=(pl.BlockSpec((None, tb, DN_DIM), lambda a, i: (a, fwd(i), 0)),
                   pl.BlockSpec((None, tb, DN_DIM), lambda a, i: (a, bwd(i), 0))),
        scratch_shapes=[pltpu.VMEM((2, DN_HEADS, DN_HEAD_DIM, DN_HEAD_DIM), F32)],
        compiler_params=_params(("parallel", "arbitrary")),
        name="delta_scan",
    )(*args, *args)


OUT_ROWS = 256


def _out_kernel(of_ref, ob_ref, zs_ref, ysc_ref, x_ref, w_ref, dnw_ref, npost_ref, nnext_ref,
                xo_ref, hn_ref, odn_ref, *, tm):
    def matmul(r0):
        rows = slice(r0, r0 + OUT_ROWS)
        for hh in range(DN_HEADS):
            sl = slice(hh * DN_HEAD_DIM, (hh + 1) * DN_HEAD_DIM)
            o = of_ref[rows, sl].astype(F32) + ob_ref[rows, sl].astype(F32)
            y = _rms(o) * dnw_ref[...] * zs_ref[rows, sl].astype(F32)
            odn_ref[rows, sl] = y.astype(odn_ref.dtype)
        return _bdot(odn_ref[rows, :], w_ref[0:DN_DIM, :]) + _bdot(ysc_ref[rows, :], w_ref[DN_DIM:, :])

    def epilogue(r0, m):
        rows = slice(r0, r0 + OUT_ROWS)
        xn = x_ref[rows, :] + _rms(m) * npost_ref[...]
        xo_ref[rows, :] = xn
        hn_ref[rows, :] = (_rms(xn) * nnext_ref[...]).astype(hn_ref.dtype)

    _pipelined([(functools.partial(matmul, r0), functools.partial(epilogue, r0))
                for r0 in range(0, tm, OUT_ROWS)])


def _out_proj(o_f, o_b, zs, ysc, x, w, layer, dn_w, n_post, n_next, tm):
    n, t, d = x.shape
    assert tm % OUT_ROWS == 0
    row = lambda width: pl.BlockSpec((None, tm, width), lambda a, i: (a, i, 0))
    vec = lambda width: pl.BlockSpec((1, width), lambda a, i: (0, 0))
    return pl.pallas_call(
        functools.partial(_out_kernel, tm=tm),
        out_shape=(jax.ShapeDtypeStruct((n, t, d), F32), jax.ShapeDtypeStruct((n, t, d), BF16)),
        grid=(n, t // tm),
        in_specs=[row(DN_DIM), row(DN_DIM), row(DN_DIM), row(SC_DIM), row(d),
                  _layer_resident(w, layer),
                  vec(DN_HEAD_DIM), vec(d), vec(d)],
        out_specs=(row(d), row(d)),
        scratch_shapes=[pltpu.VMEM((tm, DN_DIM), BF16)],
        compiler_params=_params(("parallel", "parallel")),
        name="out_proj",
    )(o_f, o_b, zs, ysc, x, w, dn_w.reshape(1, -1), n_post.reshape(1, d), n_next.reshape(1, d))


def _ffn_kernel(h_ref, hp_ref, hn_ref, wa_ref, wb_ref, cw_ref, wd_ref, x_ref, npost_ref,
                xo_ref, acc_ref, *, tm):
    i = pl.program_id(1)
    f = pl.program_id(2)

    @pl.when(f == 0)
    def _():
        acc_ref[...] = jnp.zeros_like(acc_ref)

    h = h_ref[...]
    hx = _with_halo(h, hp_ref[...], hn_ref[...], i == 0, i == pl.num_programs(1) - 1)
    a = _bdot(hx, wa_ref[...])
    b = _bdot(h, wb_ref[...])
    act = (_silu(_conv3_rows(a, cw_ref[...], tm)) * b).astype(BF16)
    acc_ref[...] += _bdot(act, wd_ref[...])

    @pl.when(f == pl.num_programs(2) - 1)
    def _():
        xo_ref[...] = x_ref[...] + _rms(acc_ref[...]) * npost_ref[...]


def _ffn(h, x, w_up, cw, w_down, layer, n_post, tm, tf=512):
    n, t, d = x.shape
    nf = D_FF // tf
    main, prev, nxt = _halo_specs(tm, t, d)
    row = lambda: pl.BlockSpec((None, tm, d), lambda a, i, f: (a, i, 0))
    vec = lambda: pl.BlockSpec((1, d), lambda a, i, f: (0, 0))
    return pl.pallas_call(
        functools.partial(_ffn_kernel, tm=tm),
        out_shape=jax.ShapeDtypeStruct((n, t, d), F32),
        grid=(n, t // tm, nf),
        in_specs=[main, prev, nxt,
                  pl.BlockSpec((None, d, tf), lambda a, i, f: (layer, 0, f)),
                  pl.BlockSpec((None, d, tf), lambda a, i, f: (layer, 0, f + nf)),
                  pl.BlockSpec((3, tf), lambda a, i, f: (0, f)),
                  pl.BlockSpec((None, tf, d), lambda a, i, f: (layer, f, 0)),
                  row(), vec()],
        out_specs=row(),
        scratch_shapes=[pltpu.VMEM((tm, d), F32)],
        compiler_params=_params(("parallel", "parallel", "arbitrary")),
        name="ffn",
    )(h, h, h, w_up, w_up, cw, w_down, x, n_post.reshape(1, d))


def _tiles(t):
    return min(512, t), min(DELTA_UNROLL * CHUNK, t)


def _win_kernel(w_ref, q_ref, zg_ref, sc_ref):
    z_end = 4 * DN_DIM
    g_end = z_end + N_GATE_COLS
    q_ref[...] = w_ref[:, :3 * DN_DIM].astype(q_ref.dtype)
    zg_ref[:, :DN_DIM] = w_ref[:, 3 * DN_DIM:z_end].astype(zg_ref.dtype)
    gate = w_ref[:, z_end:z_end + LANES]
    lane = lax.broadcasted_iota(jnp.int32, gate.shape, 1)
    zg_ref[:, DN_DIM:] = jnp.where(lane < N_GATE_COLS, gate, 0.0).astype(zg_ref.dtype)
    gd = SC_GROUP_DIM
    for g in range(SC_GROUPS):
        for seg in range(3):
            src = g_end + seg * SC_DIM + g * gd
            sc_ref[:, (3 * g + seg) * gd:(3 * g + seg + 1) * gd] = w_ref[:, src:src + gd].astype(sc_ref.dtype)


def _prepare_w_in(w_in, tk=256):
    depth, d, cols = w_in.shape
    out = lambda width: jax.ShapeDtypeStruct((depth, d, width), BF16)
    spec = lambda width: pl.BlockSpec((None, tk, width), lambda l, i: (l, i, 0))
    return pl.pallas_call(
        _win_kernel,
        out_shape=(out(3 * DN_DIM), out(DN_DIM + LANES), out(3 * SC_DIM)),
        grid=(depth, d // tk),
        in_specs=[spec(cols)],
        out_specs=(spec(3 * DN_DIM), spec(DN_DIM + LANES), spec(3 * SC_DIM)),
        compiler_params=_params(("parallel", "parallel")),
        name="w_in_prep",
    )(w_in)


def _prepare(w_in, a_log, dt_bias):
    depth = w_in.shape[0]
    w_qkv, w_zg, w_sc = _prepare_w_in(w_in)
    vpad = jnp.zeros((depth, 1, LANES - 2 * DN_HEADS), F32)
    alog = jnp.concatenate([a_log.reshape(depth, 1, 2 * DN_HEADS).astype(F32), vpad], axis=-1)
    dtb = jnp.concatenate([dt_bias.reshape(depth, 1, 2 * DN_HEADS).astype(F32), vpad], axis=-1)
    return w_qkv, w_zg, w_sc, alog, dtb


def _trunk(x, prm):
    (norm_mix_pre, w_qkv, w_zg, w_sc, alog, dtb, conv_qkv, dn_norm, conv_sc, sc_norm, w_out,
     norm_mix_post, norm_ffn_pre, w_up, conv_ffn, w_down, norm_ffn_post) = prm
    depth = w_qkv.shape[0]
    n, t, _ = x.shape
    tm, tb = _tiles(t)
    for l in range(depth):
        qkv, zs, gates, ysc, k_t, gates_t = _in_proj(x, norm_mix_pre[l], w_qkv, w_zg, w_sc, l,
                                                     conv_qkv[l], conv_sc[l], sc_norm[l], alog[l], dtb[l], tm)
        gates_t = gates_t.reshape(n, t // CHUNK, N_GATE_COLS // 2, 2 * CHUNK)
        o_f, o_b = _delta_scan(qkv, gates, gates_t, k_t, tb)
        x, h = _out_proj(o_f, o_b, zs, ysc, x, w_out, l, dn_norm[l], norm_mix_post[l], norm_ffn_pre[l], tm)
        x = _ffn(h, x, w_up, conv_ffn[l], w_down, l, norm_ffn_post[l], tm)
    return x


def kernel(x_prompt, x_sample, norm_mix_pre, w_in, conv_qkv, a_log, dt_bias, dn_norm, conv_sc, sc_norm, w_out, norm_mix_post, norm_ffn_pre, w_up, conv_ffn, w_down, norm_ffn_post):
    w_qkv, w_zg, w_sc, alog, dtb = _prepare(w_in, a_log, dt_bias)
    prm = (norm_mix_pre, w_qkv, w_zg, w_sc, alog, dtb, conv_qkv, dn_norm, conv_sc, sc_norm,
           w_out.astype(BF16), norm_mix_post, norm_ffn_pre, w_up.astype(BF16), conv_ffn,
           w_down.astype(BF16), norm_ffn_post)
    return (_trunk(x_prompt, prm), _trunk(x_sample, prm))
```
